```python
import math
import jax, jax.numpy as jnp
from jax import lax
import numpy as np

D_MODEL = 1024
BATCH = 8
SEQ = 2048
DEPTH = 4
DEC_BATCH = 32
DEC_SEQ = 1
PAST_LEN = 8192
PAGE_SIZE = 128

HEAD_DIM = 128
H_M = 4
H_F = 4
H_B = 4
W_M = H_M * HEAD_DIM
W_F = H_F * HEAD_DIM
W_B = H_B * HEAD_DIM
D_FF = ((8 * D_MODEL // 3 + 127) // 128) * 128
CONV_W = 4
MLSTM_CHUNK = 64
FOX_QBLOCK = 128
MOBA_BLOCK = 256
MOBA_TOPK = 3
MOBA_QCHUNK = 8
EPS = 1e-6

_SPLITS = (("m_q", W_M), ("m_k", W_M), ("m_v", W_M), ("m_o", W_M), ("m_i", H_M), ("m_f", H_M),
           ("f_q", W_F), ("f_k", W_F), ("f_v", W_F), ("f_f", H_F),
           ("b_q", W_B), ("b_k", W_B), ("b_v", W_B),
           ("g_m", D_MODEL), ("g_f", D_MODEL), ("g_b", D_MODEL))
N_IN = 4 * W_M + 2 * H_M + 3 * W_F + H_F + 3 * W_B + 3 * D_MODEL

kernel_name = "hybrid_mlstm_fox_moba_decode_step"


def rmsnorm(x, g):
    xf = x.astype(jnp.float32)
    y = xf * lax.rsqrt(jnp.mean(xf * xf, -1, keepdims=True) + EPS)
    return (y * g.astype(jnp.float32)).astype(x.dtype)


def swiglu(x, w_gu, w_d):
    g, u = jnp.split(x @ w_gu, 2, axis=-1)
    return (jax.nn.silu(g) * u) @ w_d


def split_cols(z):
    out, off = {}, 0
    for name, w in _SPLITS:
        out[name] = z[..., off:off + w]
        off += w
    return out


def causal_conv(x, prev, w):
    T = x.shape[1]
    xp = jnp.concatenate([prev.astype(x.dtype), x], axis=1)
    y = sum(w[j] * xp[:, j:j + T] for j in range(CONV_W))
    return y, xp[:, -(CONV_W - 1):]


def mlstm_chunked(q, k, v, i_pre, logf, C0, n0, m0, chunk):
    B, T, H, D = q.shape
    nc = T // chunk

    def to_chunks(a):
        a = jnp.moveaxis(a.reshape((B, nc, chunk) + a.shape[2:]), 1, 0)
        return jnp.swapaxes(a, 2, 3)

    causal = jnp.tril(jnp.ones((chunk, chunk), bool))

    def step(carry, xs):
        C, n, m = carry
        qc, kc, vc, ic, fc = xs
        b = jnp.cumsum(fc, axis=-1)
        dmat = jnp.where(causal, b[..., :, None] - b[..., None, :] + ic[..., None, :], -jnp.inf)
        inter = b + m[..., None]
        m_t = jnp.maximum(inter, jnp.max(dmat, -1))
        w_inter = jnp.exp(inter - m_t)
        s = jnp.einsum('bhtd,bhsd->bhts', qc, kc) * jnp.exp(dmat - m_t[..., None])
        num = w_inter[..., None] * jnp.einsum('bhtd,bhde->bhte', qc, C) + jnp.einsum('bhts,bhse->bhte', s, vc)
        den = w_inter * jnp.einsum('bhtd,bhd->bht', qc, n) + jnp.sum(s, -1)
        h = num / jnp.maximum(jnp.abs(den), jnp.exp(-m_t))[..., None]
        bL = b[..., -1]
        gdec = bL[..., None] - b + ic
        m_new = jnp.maximum(bL + m, jnp.max(gdec, -1))
        a_prev = jnp.exp(bL + m - m_new)
        wg = jnp.exp(gdec - m_new[..., None])
        C_new = a_prev[..., None, None] * C + jnp.einsum('bhs,bhsd,bhse->bhde', wg, kc, vc)
        n_new = a_prev[..., None] * n + jnp.einsum('bhs,bhsd->bhd', wg, kc)
        return (C_new, n_new, m_new), h

    (C, n, m), hs = lax.scan(step, (C0, n0, m0), tuple(map(to_chunks, (q, k, v, i_pre, logf))))
    hs = jnp.moveaxis(jnp.swapaxes(hs, 2, 3), 0, 1).reshape(B, T, H, D)
    return hs, C, n, m


def mlstm_branch(p, b_i, b_f, w_conv, g_hn, conv_prev, C0, n0, m0):
    B, T, _ = p["m_v"].shape
    f32 = jnp.float32
    qk, conv_new = causal_conv(jnp.concatenate([p["m_q"], p["m_k"]], -1), conv_prev, w_conv)
    qk = jax.nn.silu(qk).astype(f32)
    q = qk[..., :W_M].reshape(B, T, H_M, HEAD_DIM)
    k = qk[..., W_M:].reshape(B, T, H_M, HEAD_DIM) * (HEAD_DIM ** -0.5)
    v = p["m_v"].astype(f32).reshape(B, T, H_M, HEAD_DIM)
    i_pre = (p["m_i"] + b_i).astype(f32)
    logf = jax.nn.log_sigmoid((p["m_f"] + b_f).astype(f32))
    h, C, n, m = mlstm_chunked(q, k, v, i_pre, logf, C0.astype(f32), n0.astype(f32), m0.astype(f32),
                               math.gcd(T, MLSTM_CHUNK))
    h = h * lax.rsqrt(jnp.mean(h * h, -1, keepdims=True) + EPS)
    h = h.reshape(B, T, W_M) * g_hn.astype(f32)
    out = (jax.nn.sigmoid(p["m_o"].astype(f32)) * h).astype(p["m_v"].dtype)
    return out, (C, n, m, conv_new)


def fox_attend(q, k, v, c_q, c_k, q_pos0, qblock):
    B, Tq, H, D = q.shape
    L = k.shape[1]
    nb = Tq // qblock
    kpos = jnp.arange(L)
    ckT = jnp.transpose(c_k, (0, 2, 1))
    qb = jnp.swapaxes(q.reshape(B, nb, qblock, H, D), 0, 1)
    cqb = jnp.swapaxes(c_q.reshape(B, nb, qblock, H), 0, 1)
    posb = (q_pos0 + jnp.arange(Tq)).reshape(nb, qblock)

    def one(args):
        qi, ci, pi = args
        s = jnp.einsum('bqhd,bkhd->bhqk', qi, k, preferred_element_type=jnp.float32) * (D ** -0.5)
        s = s + jnp.transpose(ci, (0, 2, 1))[..., None] - ckT[:, :, None, :]
        s = jnp.where(kpos[None, :] <= pi[:, None], s, -jnp.inf)
        pr = jax.nn.softmax(s, axis=-1)
        return jnp.einsum('bhqk,bkhd->bqhd', pr.astype(v.dtype), v)

    o = lax.map(one, (qb, cqb, posb))
    return jnp.swapaxes(o, 0, 1).reshape(B, Tq, H, D)


def moba_attend(q, k, v, q_pos0, qchunk):
    B, Tq, H, D = q.shape
    L = k.shape[1]
    nblk = -(-L // MOBA_BLOCK)
    pad = nblk * MOBA_BLOCK - L
    padw = ((0, 0), (0, pad), (0, 0), (0, 0))
    kb = jnp.transpose(jnp.pad(k, padw).reshape(B, nblk, MOBA_BLOCK, H, D), (0, 3, 1, 2, 4))
    vb = jnp.transpose(jnp.pad(v, padw).reshape(B, nblk, MOBA_BLOCK, H, D), (0, 3, 1, 2, 4))
    kmean = jnp.mean(kb.astype(jnp.float32), axis=3)
    ksel = min(MOBA_TOPK, nblk)
    nq = Tq // qchunk
    qc = jnp.transpose(q.reshape(B, nq, qchunk, H, D), (1, 0, 3, 2, 4))
    posc = (q_pos0 + jnp.arange(Tq)).reshape(nq, qchunk)
    blk_ids = jnp.arange(nblk)
    offs = jnp.arange(MOBA_BLOCK)
    gather = jax.vmap(jax.vmap(lambda tab, ix: tab[ix]))

    def one(args):
        qi, pi = args
        own = (pi // MOBA_BLOCK).astype(jnp.int32)
        gs = jnp.einsum('bhqd,bhnd->bhqn', qi.astype(jnp.float32), kmean)
        gs = jnp.where(blk_ids[None, :] < own[:, None], gs, -jnp.inf)
        top_s, top_i = lax.top_k(gs, ksel)
        idx = jnp.concatenate([top_i.astype(jnp.int32),
                               jnp.broadcast_to(own[None, None, :, None], (B, H, qchunk, 1))], -1)
        slot_ok = jnp.concatenate([jnp.isfinite(top_s), jnp.ones((B, H, qchunk, 1), bool)], -1)
        kg = gather(kb, idx)
        vg = gather(vb, idx)
        s = jnp.einsum('bhqd,bhqsrd->bhqsr', qi, kg, preferred_element_type=jnp.float32) * (D ** -0.5)
        kpos = idx[..., None] * MOBA_BLOCK + offs
        valid = slot_ok[..., None] & (kpos <= pi[:, None, None])
        s = jnp.where(valid, s, -jnp.inf)
        pr = jax.nn.softmax(s.reshape(B, H, qchunk, -1), axis=-1).reshape(s.shape)
        return jnp.einsum('bhqsr,bhqsrd->bhqd', pr.astype(v.dtype), vg)

    o = lax.map(one, (qc, posc))
    return jnp.transpose(o, (1, 0, 3, 2, 4)).reshape(B, Tq, H, D)


def layer(x, w, conv_prev, C0, n0, m0, past, q_pos0):
    B, T, _ = x.shape
    g = w["norm_g"]
    x = x + 0.5 * rmsnorm(swiglu(rmsnorm(x, g[0]), w["ffn1_gu"], w["ffn1_d"]), g[1])
    h = rmsnorm(x, g[2])
    p = split_cols(h @ w["in"])
    m_out, (C, n, m, conv_new) = mlstm_branch(p, w["b_m_i"], w["b_m_f"], w["conv"], w["hn"], conv_prev, C0, n0, m0)
    fq = p["f_q"].reshape(B, T, H_F, HEAD_DIM)
    fk = p["f_k"].reshape(B, T, H_F, HEAD_DIM)
    fv = p["f_v"].reshape(B, T, H_F, HEAD_DIM)
    flogf = jax.nn.log_sigmoid((p["f_f"] + w["b_fox_f"]).astype(jnp.float32))
    bq = p["b_q"].reshape(B, T, H_B, HEAD_DIM)
    bk = p["b_k"].reshape(B, T, H_B, HEAD_DIM)
    bv = p["b_v"].reshape(B, T, H_B, HEAD_DIM)
    if past is None:
        fk_all, fv_all, lf_all, bk_all, bv_all = fk, fv, flogf, bk, bv
    else:
        pk, pv, plf, pbk, pbv = past
        fk_all = jnp.concatenate([pk.astype(fk.dtype), fk], 1)
        fv_all = jnp.concatenate([pv.astype(fv.dtype), fv], 1)
        lf_all = jnp.concatenate([plf.astype(jnp.float32), flogf], 1)
        bk_all = jnp.concatenate([pbk.astype(bk.dtype), bk], 1)
        bv_all = jnp.concatenate([pbv.astype(bv.dtype), bv], 1)
    c_all = lf_all - jnp.cumsum(lf_all[:, ::-1], axis=1)[:, ::-1]
    f_out = fox_attend(fq, fk_all, fv_all, c_all[:, -T:], c_all, q_pos0, math.gcd(T, FOX_QBLOCK))
    b_out = moba_attend(bq, bk_all, bv_all, q_pos0, math.gcd(T, MOBA_QCHUNK))
    merged = (jax.nn.sigmoid(p["g_m"]) * (m_out @ w["br_m"])
              + jax.nn.sigmoid(p["g_f"]) * (f_out.reshape(B, T, W_F) @ w["br_f"])
              + jax.nn.sigmoid(p["g_b"]) * (b_out.reshape(B, T, W_B) @ w["br_b"]))
    x = x + rmsnorm(merged @ w["out"], g[3])
    x = x + 0.5 * rmsnorm(swiglu(rmsnorm(x, g[4]), w["ffn2_gu"], w["ffn2_d"]), g[5])
    return x, (fk, fv, flogf, bk, bv, C, n, m, conv_new)


def setup_inputs(seed: int = 0) -> dict:
    key = jax.random.key(seed)
    ks = jax.random.split(key, 32)
    f32 = jnp.float32

    def nrm(i, shape, scale):
        return jax.random.normal(ks[i], shape, f32) * scale

    n_pages = PAST_LEN // PAGE_SIZE
    n_used = DEC_BATCH * n_pages
    n_pool = n_used + -(-n_used // 4)
    page_table = jax.random.permutation(ks[0], n_pool)[:n_used].reshape(DEC_BATCH, n_pages).astype(jnp.int32)
    return {
        "x_prompt": nrm(1, (BATCH, SEQ, D_MODEL), 1.0),
        "x_sample": nrm(2, (DEC_BATCH, DEC_SEQ, D_MODEL), 1.0),
        "cache_fox_k": nrm(3, (DEPTH, n_pool, PAGE_SIZE, H_F, HEAD_DIM), 1.0),
        "cache_fox_v": nrm(4, (DEPTH, n_pool, PAGE_SIZE, H_F, HEAD_DIM), 1.0),
        "cache_fox_logf": jax.nn.log_sigmoid(2.0 + nrm(5, (DEPTH, n_pool, PAGE_SIZE, H_F), 1.0)),
        "cache_moba_k": nrm(6, (DEPTH, n_pool, PAGE_SIZE, H_B, HEAD_DIM), 1.0),
        "cache_moba_v": nrm(7, (DEPTH, n_pool, PAGE_SIZE, H_B, HEAD_DIM), 1.0),
        "state_mlstm_C": nrm(8, (DEPTH, DEC_BATCH, H_M, HEAD_DIM, HEAD_DIM), HEAD_DIM ** -0.5),
        "state_mlstm_n": nrm(9, (DEPTH, DEC_BATCH, H_M, HEAD_DIM), 1.0),
        "state_mlstm_m": nrm(10, (DEPTH, DEC_BATCH, H_M), 1.0),
        "state_mlstm_conv": nrm(11, (DEPTH, DEC_BATCH, CONV_W - 1, 2 * W_M), 1.0),
        "page_table": page_table,
        "norm_g": 1.0 + nrm(12, (DEPTH, 6, D_MODEL), 0.05),
        "w_ffn1_gu": nrm(13, (DEPTH, D_MODEL, 2 * D_FF), D_MODEL ** -0.5),
        "w_ffn1_d": nrm(14, (DEPTH, D_FF, D_MODEL), D_FF ** -0.5),
        "w_ffn2_gu": nrm(15, (DEPTH, D_MODEL, 2 * D_FF), D_MODEL ** -0.5),
        "w_ffn2_d": nrm(16, (DEPTH, D_FF, D_MODEL), D_FF ** -0.5),
        "w_in": nrm(17, (DEPTH, D_MODEL, N_IN), D_MODEL ** -0.5),
        "b_mlstm_i": nrm(18, (DEPTH, H_M), 0.1),
        "b_mlstm_f": jnp.linspace(3.0, 6.0, H_M, dtype=f32)[None, :] + nrm(19, (DEPTH, H_M), 0.1),
        "b_fox_f": 2.0 + nrm(20, (DEPTH, H_F), 0.5),
        "w_conv": nrm(21, (DEPTH, CONV_W, 2 * W_M), CONV_W ** -0.5),
        "g_headnorm": 1.0 + nrm(22, (DEPTH, W_M), 0.05),
        "w_br_m": nrm(23, (DEPTH, W_M, D_MODEL), W_M ** -0.5),
        "w_br_f": nrm(24, (DEPTH, W_F, D_MODEL), W_F ** -0.5),
        "w_br_b": nrm(25, (DEPTH, W_B, D_MODEL), W_B ** -0.5),
        "w_out": nrm(26, (DEPTH, D_MODEL, D_MODEL), D_MODEL ** -0.5),
    }


def reference(x_prompt, x_sample, cache_fox_k, cache_fox_v, cache_fox_logf, cache_moba_k, cache_moba_v,
              state_mlstm_C, state_mlstm_n, state_mlstm_m, state_mlstm_conv, page_table,
              norm_g, w_ffn1_gu, w_ffn1_d, w_ffn2_gu, w_ffn2_d, w_in, b_mlstm_i, b_mlstm_f, b_fox_f,
              w_conv, g_headnorm, w_br_m, w_br_f, w_br_b, w_out):
    B = x_prompt.shape[0]
    Bs = x_sample.shape[0]
    P = page_table.shape[1] * PAGE_SIZE
    yp, ys = x_prompt, x_sample
    new_p, new_s = [], []
    for l in range(DEPTH):
        w = {"norm_g": norm_g[l], "ffn1_gu": w_ffn1_gu[l], "ffn1_d": w_ffn1_d[l],
             "ffn2_gu": w_ffn2_gu[l], "ffn2_d": w_ffn2_d[l], "in": w_in[l],
             "b_m_i": b_mlstm_i[l], "b_m_f": b_mlstm_f[l], "b_fox_f": b_fox_f[l],
             "conv": w_conv[l], "hn": g_headnorm[l], "br_m": w_br_m[l], "br_f": w_br_f[l],
             "br_b": w_br_b[l], "out": w_out[l]}
        yp, st_p = layer(yp, w,
                         jnp.zeros((B, CONV_W - 1, 2 * W_M), yp.dtype),
                         jnp.zeros((B, H_M, HEAD_DIM, HEAD_DIM), jnp.float32),
                         jnp.zeros((B, H_M, HEAD_DIM), jnp.float32),
                         jnp.full((B, H_M), -jnp.inf, jnp.float32),
                         None, 0)
        new_p.append(st_p)
        past = (cache_fox_k[l, page_table].reshape(Bs, P, H_F, HEAD_DIM),
                cache_fox_v[l, page_table].reshape(Bs, P, H_F, HEAD_DIM),
                cache_fox_logf[l, page_table].reshape(Bs, P, H_F),
                cache_moba_k[l, page_table].reshape(Bs, P, H_B, HEAD_DIM),
                cache_moba_v[l, page_table].reshape(Bs, P, H_B, HEAD_DIM))
        ys, st_s = layer(ys, w, state_mlstm_conv[l], state_mlstm_C[l], state_mlstm_n[l], state_mlstm_m[l], past, P)
        new_s.append(st_s)
    (p_fox_k, p_fox_v, p_fox_logf, p_moba_k, p_moba_v,
     p_mlstm_C, p_mlstm_n, p_mlstm_m, p_mlstm_conv) = [jnp.stack(z) for z in zip(*new_p)]
    (s_fox_k, s_fox_v, s_fox_logf, s_moba_k, s_moba_v,
     s_mlstm_C, s_mlstm_n, s_mlstm_m, s_mlstm_conv) = [jnp.stack(z) for z in zip(*new_s)]
    return (yp, ys,
            p_fox_k, p_fox_v, p_fox_logf, p_moba_k, p_moba_v, p_mlstm_C, p_mlstm_n, p_mlstm_m, p_mlstm_conv,
            s_fox_k, s_fox_v, s_fox_logf, s_moba_k, s_moba_v, s_mlstm_C, s_mlstm_n, s_mlstm_m, s_mlstm_conv)
```

```python
import functools
import math

import jax
import jax.numpy as jnp
from jax import lax
from jax.experimental import pallas as pl
from jax.experimental.pallas import tpu as pltpu

F32 = jnp.float32
BF16 = jnp.bfloat16
EPS = 1e-6
NEG = -1e30

HEAD_DIM = 128
N_HEADS = 4
W_MIX = N_HEADS * HEAD_DIM
CONV_W = 4
MOBA_BLOCK = 256
MOBA_TOPK = 3
PAGE = 128
LANES = 128
VMEM_LIMIT = 52 * 1024 * 1024


def _cparams(*sem):
    return pltpu.CompilerParams(dimension_semantics=sem, vmem_limit_bytes=VMEM_LIMIT)


def _rms(x, g):
    return x * lax.rsqrt(jnp.mean(x * x, axis=-1, keepdims=True) + EPS) * g


def _split3(a):
    hi = a.astype(BF16)
    r = a - hi.astype(F32)
    mid = r.astype(BF16)
    lo = (r - mid.astype(F32)).astype(BF16)
    return hi, mid, lo


def _dot(a, b):
    return jnp.dot(a, b, preferred_element_type=F32)


def _dot_nt(a, b):
    return lax.dot_general(a, b, (((1,), (1,)), ((), ())), preferred_element_type=F32)


def _dot3_l(a_f32, b_bf16):
    hi, mid, lo = _split3(a_f32)
    return _dot(hi, b_bf16) + _dot(mid, b_bf16) + _dot(lo, b_bf16)


def _dot3_r(a_bf16, b_f32):
    hi, mid, lo = _split3(b_f32)
    return _dot(a_bf16, hi) + _dot(a_bf16, mid) + _dot(a_bf16, lo)


def _log_sigmoid(z):
    return jnp.minimum(z, 0.0) - jnp.log1p(jnp.exp(-jnp.abs(z)))


def _sigmoid(z):
    return 1.0 / (1.0 + jnp.exp(-z))


def _ffn_kernel(x_ref, ga_ref, gb_ref, wg_ref, wu_ref, wd_ref, o_ref, xn_ref, acc_ref):
    j = pl.program_id(1)

    @pl.when(j == 0)
    def _():
        xn_ref[...] = _rms(x_ref[...], ga_ref[...]).astype(BF16)
        acc_ref[...] = jnp.zeros_like(acc_ref)

    xn = xn_ref[...]
    g = _dot(xn, wg_ref[...])
    u = _dot(xn, wu_ref[...])
    h = (g * _sigmoid(g) * u).astype(BF16)
    acc_ref[...] += _dot(h, wd_ref[...])

    @pl.when(j == pl.num_programs(1) - 1)
    def _():
        o_ref[...] = x_ref[...] + 0.5 * _rms(acc_ref[...], gb_ref[...])


def _ffn_block(x, ga, gb, w_gu, w_d, *, tm, tf):
    m, d = x.shape
    f = w_d.shape[0]
    nj = f // tf
    return pl.pallas_call(
        _ffn_kernel,
        grid=(m // tm, nj),
        in_specs=[
            pl.BlockSpec((tm, d), lambda i, j: (i, 0)),
            pl.BlockSpec((1, d), lambda i, j: (0, 0)),
            pl.BlockSpec((1, d), lambda i, j: (0, 0)),
            pl.BlockSpec((d, tf), lambda i, j: (0, j)),
            pl.BlockSpec((d, tf), lambda i, j: (0, j + nj)),
            pl.BlockSpec((tf, d), lambda i, j: (j, 0)),
        ],
        out_specs=pl.BlockSpec((tm, d), lambda i, j: (i, 0)),
        out_shape=jax.ShapeDtypeStruct((m, d), F32),
        scratch_shapes=[pltpu.VMEM((tm, d), BF16), pltpu.VMEM((tm, d), F32)],
        compiler_params=_cparams("parallel", "arbitrary"),
        name="ffn_block",
    )(x, ga, gb, w_gu, w_gu, w_d)


N_BF_BLOCKS = 12
N_F32_BLOCKS = 4


def _inproj_kernel(x_ref, g_ref, w_ref, wgate_ref, pbf_ref, fk_ref, fv_ref, bk_ref, bv_ref,
                   gate_ref, *rest, with_gate_t):
    if with_gate_t:
        gate_t_ref, xn_ref = rest
    else:
        (xn_ref,) = rest
    j = pl.program_id(1)

    @pl.when(j == 0)
    def _():
        xn = _rms(x_ref[...], g_ref[...]).astype(BF16)
        xn_ref[...] = xn
        gate = _dot(xn, wgate_ref[...])
        gate_ref[...] = gate
        if with_gate_t:
            gate_t_ref[...] = gate.T

    y = _dot(xn_ref[...], w_ref[...])

    @pl.when(j < N_BF_BLOCKS)
    def _():
        pbf_ref[...] = y.astype(BF16)

    for k, ref in enumerate((fk_ref, fv_ref, bk_ref, bv_ref)):
        @pl.when(j == N_BF_BLOCKS + k)
        def _(ref=ref):
            ref[...] = y


def _inproj(x, g, w_all, w_gate, *, tm, with_gate_t):
    m, d = x.shape
    nb = N_BF_BLOCKS + N_F32_BLOCKS
    wide = pl.BlockSpec((tm, W_MIX), lambda i, j: (i, 0))
    out_shape = [
        jax.ShapeDtypeStruct((m, N_BF_BLOCKS * W_MIX), BF16),
        jax.ShapeDtypeStruct((m, W_MIX), F32), jax.ShapeDtypeStruct((m, W_MIX), F32),
        jax.ShapeDtypeStruct((m, W_MIX), F32), jax.ShapeDtypeStruct((m, W_MIX), F32),
        jax.ShapeDtypeStruct((m, LANES), F32),
    ]
    out_specs = [
        pl.BlockSpec((tm, W_MIX), lambda i, j: (i, jnp.minimum(j, N_BF_BLOCKS - 1))),
        wide, wide, wide, wide,
        pl.BlockSpec((tm, LANES), lambda i, j: (i, 0)),
    ]
    if with_gate_t:
        out_shape.append(jax.ShapeDtypeStruct((LANES, m), F32))
        out_specs.append(pl.BlockSpec((LANES, tm), lambda i, j: (0, i)))
    return pl.pallas_call(
        functools.partial(_inproj_kernel, with_gate_t=with_gate_t),
        grid=(m // tm, nb),
        in_specs=[
            pl.BlockSpec((tm, d), lambda i, j: (i, 0)),
            pl.BlockSpec((1, d), lambda i, j: (0, 0)),
            pl.BlockSpec((d, W_MIX), lambda i, j: (0, j)),
            pl.BlockSpec((d, LANES), lambda i, j: (0, 0)),
        ],
        out_specs=out_specs,
        out_shape=out_shape,
        scratch_shapes=[pltpu.VMEM((tm, d), BF16)],
        compiler_params=_cparams("parallel", "arbitrary"),
        name="inproj",
    )(x, g, w_all, w_gate)


def _merge_kernel(x_ref, mo_ref, fo_ref, bo_ref, gm_ref, gf_ref, gb_ref, g_ref,
                  wm_ref, wf_ref, wb_ref, wo_ref, o_ref):
    merged = (_sigmoid(gm_ref[...].astype(F32)) * _dot(mo_ref[...], wm_ref[...])
              + _sigmoid(gf_ref[...].astype(F32)) * _dot(fo_ref[...], wf_ref[...])
              + _sigmoid(gb_ref[...].astype(F32)) * _dot(bo_ref[...], wb_ref[...]))
    y = _dot(merged.astype(BF16), wo_ref[...])
    o_ref[...] = x_ref[...] + _rms(y, g_ref[...])


def _merge_out(x, m_out, f_out, b_out, pbf, g, w_m, w_f, w_b, w_o, *, tm):
    m, d = x.shape
    gate_blk = d // W_MIX

    def gate_spec(k):
        return pl.BlockSpec((tm, d), lambda i: (i, (6 + k * gate_blk) // gate_blk))

    row = pl.BlockSpec((tm, W_MIX), lambda i: (i, 0))
    wbr = pl.BlockSpec((W_MIX, d), lambda i: (0, 0))
    return pl.pallas_call(
        _merge_kernel,
        grid=(m // tm,),
        in_specs=[
            pl.BlockSpec((tm, d), lambda i: (i, 0)), row, row, row,
            gate_spec(0), gate_spec(1), gate_spec(2),
            pl.BlockSpec((1, d), lambda i: (0, 0)),
            wbr, wbr, wbr, pl.BlockSpec((d, d), lambda i: (0, 0)),
        ],
        out_specs=pl.BlockSpec((tm, d), lambda i: (i, 0)),
        out_shape=jax.ShapeDtypeStruct((m, d), F32),
        compiler_params=_cparams("parallel"),
        name="merge_out",
    )(x, m_out, f_out, b_out, pbf, pbf, pbf, g, w_m, w_f, w_b, w_o)


def _w_in_offsets(d_model):
    widths = (("m_q", W_MIX), ("m_k", W_MIX), ("m_v", W_MIX), ("m_o", W_MIX), ("m_i", N_HEADS), ("m_f", N_HEADS),
              ("f_q", W_MIX), ("f_k", W_MIX), ("f_v", W_MIX), ("f_f", N_HEADS),
              ("b_q", W_MIX), ("b_k", W_MIX), ("b_v", W_MIX),
              ("g_m", d_model), ("g_f", d_model), ("g_b", d_model))
    out, off = {}, 0
    for name, w in widths:
        out[name] = (off, off + w)
        off += w
    return out


def _prep_w_in(w_in):
    d = w_in.shape[-2]
    offs = _w_in_offsets(d)

    def cols(name):
        a, b = offs[name]
        return w_in[..., a:b]

    order = ("m_q", "m_k", "m_v", "m_o", "f_q", "b_q", "g_m", "g_f", "g_b", "f_k", "f_v", "b_k", "b_v")
    w_all = jnp.concatenate([cols(n) for n in order], axis=-1).astype(BF16)
    gate = jnp.concatenate([cols("m_i"), cols("m_f"), cols("f_f")], axis=-1)
    pad = [(0, 0)] * (gate.ndim - 1) + [(0, LANES - gate.shape[-1])]
    w_gate = jnp.pad(gate, pad).astype(BF16)
    return w_all, w_gate


def _gateprep_kernel(gate_ref, gate_t_ref, bias_ref, bias_t_ref,
                     act_ref, cl_ref, cf_ref, act_t_ref, cl_t_ref, cf_t_ref, *, chunk):
    t = gate_ref.shape[0]
    lane = lax.broadcasted_iota(jnp.int32, (1, LANES), 1)
    z = gate_ref[...] + bias_ref[...]
    act_ref[...] = jnp.where(lane < N_HEADS, z, _log_sigmoid(z))
    sub = lax.broadcasted_iota(jnp.int32, (LANES, 1), 0)
    zt = gate_t_ref[...] + bias_t_ref[...]
    act_t_ref[...] = jnp.where(sub < N_HEADS, zt, _log_sigmoid(zt))

    r = lax.broadcasted_iota(jnp.int32, (chunk, chunk), 0)
    c = lax.broadcasted_iota(jnp.int32, (chunk, chunk), 1)
    tril = jnp.where(c <= r, 1.0, 0.0).astype(BF16)
    triu = jnp.where(r <= c, 1.0, 0.0).astype(BF16)
    carry = jnp.zeros((1, LANES), F32)
    carry_t = jnp.zeros((LANES, 1), F32)
    for ci in range(t // chunk):
        sl = slice(ci * chunk, (ci + 1) * chunk)
        loc = _dot3_r(tril, act_ref[sl, :])
        cl_ref[sl, :] = loc
        full = loc + carry
        cf_ref[sl, :] = full
        carry = full[chunk - 1:chunk, :]
        loc_t = _dot3_l(act_t_ref[:, sl], triu)
        cl_t_ref[:, sl] = loc_t
        full_t = loc_t + carry_t
        cf_t_ref[:, sl] = full_t
        carry_t = full_t[:, chunk - 1:chunk]


def _gateprep(gate, gate_t, bias, bias_t, *, n_seq, chunk):
    t = gate.shape[0] // n_seq
    col = pl.BlockSpec((t, LANES), lambda b: (b, 0))
    row = pl.BlockSpec((LANES, t), lambda b: (0, b))
    col_s = jax.ShapeDtypeStruct(gate.shape, F32)
    row_s = jax.ShapeDtypeStruct(gate_t.shape, F32)
    return pl.pallas_call(
        functools.partial(_gateprep_kernel, chunk=chunk),
        grid=(n_seq,),
        in_specs=[col, row, pl.BlockSpec((1, LANES), lambda b: (0, 0)), pl.BlockSpec((LANES, 1), lambda b: (0, 0))],
        out_specs=[col, col, col, row, row, row],
        out_shape=[col_s, col_s, col_s, row_s, row_s, row_s],
        compiler_params=_cparams("parallel"),
        name="gateprep",
    )(gate, gate_t, bias, bias_t)


def _mlstm_kernel(q_ref, k_ref, v_ref, o_ref, wq_ref, wk_ref, hn_ref, act_ref, cl_ref, act_t_ref, cl_t_ref,
                  out_ref, caug_ref, m_ref, cbuf, qs, kts, icb, bcb, cst, *, chunk):
    h = pl.program_id(1)
    t = q_ref.shape[0]
    reps = chunk // LANES

    def wide(a, n):
        return jnp.concatenate([a] * n, axis=1) if n > 1 else a

    def conv_silu(x_ref, w_ref):
        cbuf[0:8, :] = jnp.zeros((8, LANES), F32)
        cbuf[8:, :] = x_ref[...].astype(F32)
        y = w_ref[0:1, :] * cbuf[pl.ds(8 - (CONV_W - 1), t), :]
        for j in range(1, CONV_W):
            y = y + w_ref[j:j + 1, :] * cbuf[pl.ds(8 - (CONV_W - 1) + j, t), :]
        return y * _sigmoid(y)

    qs[...] = conv_silu(q_ref, wq_ref).astype(BF16)
    kts[...] = (conv_silu(k_ref, wk_ref) * (HEAD_DIM ** -0.5)).T.astype(BF16)

    rr = lax.broadcasted_iota(jnp.int32, (LANES, LANES), 0)
    icb[...] = _dot3_l(act_ref[...], jnp.where(rr == h, 1.0, 0.0).astype(BF16))
    bcb[...] = _dot3_l(cl_ref[...], jnp.where(rr == N_HEADS + h, 1.0, 0.0).astype(BF16))

    cst[...] = jnp.zeros_like(cst)
    tt = lax.broadcasted_iota(jnp.int32, (chunk, chunk), 0)
    ss = lax.broadcasted_iota(jnp.int32, (chunk, chunk), 1)
    ones = jnp.ones((chunk, LANES), BF16)

    def body(c, m_prev):
        r0 = pl.multiple_of(c * chunk, chunk)
        rows = pl.ds(r0, chunk)
        qc = qs[rows, :]
        ktc = kts[:, rows]
        v_aug = jnp.concatenate([v_ref[rows, :], ones], axis=1)
        i_r = act_t_ref[pl.ds(h, 1), rows]
        b_r = cl_t_ref[pl.ds(N_HEADS + h, 1), rows]
        i_c = icb[rows, :]
        b_c = bcb[rows, :]
        dmat = jnp.where(ss <= tt, wide(b_c, reps) + (i_r - b_r), -jnp.inf)
        inter = b_c + m_prev
        m_t = jnp.maximum(inter, jnp.max(dmat, axis=-1, keepdims=True))
        w_inter = jnp.exp(inter - m_t)
        s = (_dot(qc, ktc) * jnp.exp(dmat - wide(m_t, reps))).astype(BF16)
        c_prev = cst[...]
        num = wide(w_inter, 2) * _dot(qc, c_prev.astype(BF16)) + _dot(s, v_aug)
        hh = num[:, :LANES] / jnp.maximum(jnp.abs(num[:, LANES:]), jnp.exp(-m_t))
        hh = hh * lax.rsqrt(jnp.mean(hh * hh, axis=-1, keepdims=True) + EPS) * hn_ref[...]
        out_ref[rows, :] = (_sigmoid(o_ref[rows, :].astype(F32)) * hh).astype(BF16)

        b_last = b_r[:, chunk - 1:chunk]
        m_new = jnp.maximum(b_last + m_prev, jnp.max(b_last - b_r + i_r, axis=-1, keepdims=True))
        a_prev = jnp.exp(b_last + m_prev - m_new)
        wg_c = jnp.exp(b_last - b_c + i_c - m_new)
        wv = (wide(wg_c, 2) * v_aug.astype(F32)).astype(BF16)
        cst[...] = a_prev * c_prev + _dot(ktc, wv)
        return m_new

    m_fin = lax.fori_loop(0, t // chunk, body, jnp.full((1, 1), -jnp.inf, F32))
    caug_ref[...] = cst[...]
    m_ref[...] = jnp.broadcast_to(m_fin, m_ref.shape)


def _mlstm_prompt(pbf, w_conv, g_hn, act, cl, act_t, cl_t, *, n_seq, chunk):
    bt = pbf.shape[0]
    t = bt // n_seq

    def head_cols(blk):
        return pl.BlockSpec((t, HEAD_DIM), lambda b, h: (b, blk * N_HEADS + h))

    col = pl.BlockSpec((t, LANES), lambda b, h: (b, 0))
    row = pl.BlockSpec((LANES, t), lambda b, h: (0, b))
    return pl.pallas_call(
        functools.partial(_mlstm_kernel, chunk=chunk),
        grid=(n_seq, N_HEADS),
        in_specs=[
            head_cols(0), head_cols(1), head_cols(2), head_cols(3),
            pl.BlockSpec((CONV_W, HEAD_DIM), lambda b, h: (0, h)),
            pl.BlockSpec((CONV_W, HEAD_DIM), lambda b, h: (0, N_HEADS + h)),
            pl.BlockSpec((1, HEAD_DIM), lambda b, h: (0, h)),
            col, col, row, row,
        ],
        out_specs=[
            pl.BlockSpec((t, HEAD_DIM), lambda b, h: (b, h)),
            pl.BlockSpec((None, None, HEAD_DIM, 2 * LANES), lambda b, h: (b, h, 0, 0)),
            pl.BlockSpec((None, None, 8, LANES), lambda b, h: (b, h, 0, 0)),
        ],
        out_shape=[
            jax.ShapeDtypeStruct((bt, W_MIX), BF16),
            jax.ShapeDtypeStruct((n_seq, N_HEADS, HEAD_DIM, 2 * LANES), F32),
            jax.ShapeDtypeStruct((n_seq, N_HEADS, 8, LANES), F32),
        ],
        scratch_shapes=[
            pltpu.VMEM((t + 8, LANES), F32), pltpu.VMEM((t, HEAD_DIM), BF16), pltpu.VMEM((HEAD_DIM, t), BF16),
            pltpu.VMEM((t, LANES), F32), pltpu.VMEM((t, LANES), F32), pltpu.VMEM((HEAD_DIM, 2 * LANES), F32),
        ],
        compiler_params=_cparams("parallel", "arbitrary"),
        name="mlstm_prompt",
    )(pbf, pbf, pbf, pbf, w_conv, w_conv, g_hn, act, cl, act_t, cl_t)


def _attn_kernel(*refs, moba):
    if moba:
        q_ref, k_ref, v_ref, o_ref, kbf, vt, kmean, selb = refs
    else:
        q_ref, k_ref, v_ref, cf_ref, cf_t_ref, o_ref, kbf, vt, fkc = refs
    h = pl.program_id(1)
    qi = pl.program_id(2)
    t = k_ref.shape[0]
    blk = q_ref.shape[0]
    n_blk = t // blk
    reps = blk // LANES
    scale = HEAD_DIM ** -0.5

    def wide(a):
        return jnp.concatenate([a] * reps, axis=1) if reps > 1 else a

    @pl.when(qi == 0)
    def _():
        k = k_ref[...]
        kbf[...] = k.astype(BF16)
        vt[...] = v_ref[...].T.astype(BF16)
        if moba:
            kmean[...] = jnp.zeros_like(kmean)
            for n in range(n_blk):
                kmean[n:n + 1, :] = jnp.mean(k[n * blk:(n + 1) * blk, :], axis=0, keepdims=True)
        else:
            rr = lax.broadcasted_iota(jnp.int32, (LANES, LANES), 0)
            fkc[...] = _dot3_l(cf_ref[...], jnp.where(rr == 2 * N_HEADS + h, 1.0, 0.0).astype(BF16))

    qb = q_ref[...]
    q0 = pl.multiple_of(qi * blk, blk)
    if moba:
        nbp = kmean.shape[0]
        row = lax.broadcasted_iota(jnp.int32, (nbp, blk), 0)
        km_hi, km_mid, km_lo = _split3(kmean[...])
        gs = _dot_nt(km_hi, qb) + _dot_nt(km_mid, qb) + _dot_nt(km_lo, qb)
        gm = jnp.where(row < qi, gs, -jnp.inf)
        for n in range(n_blk - 1):
            g_n = gm[n:n + 1, :]
            beats = (gm > g_n) | ((gm == g_n) & (row < n))
            cnt = jnp.sum(jnp.where(beats, 1.0, 0.0), axis=0, keepdims=True)
            selb[n:n + 1, :] = jnp.where(cnt < MOBA_TOPK, 0.0, NEG)
    else:
        fq_r = cf_t_ref[pl.ds(2 * N_HEADS + h, 1), pl.ds(q0, blk)]

    def scores(j0):
        rows = pl.ds(j0, blk)
        s = _dot_nt(kbf[rows, :], qb) * scale
        if not moba:
            s = s + (fq_r - wide(fkc[rows, :]))
        return s

    ks = lax.broadcasted_iota(jnp.int32, (blk, blk), 0)
    qs = lax.broadcasted_iota(jnp.int32, (blk, blk), 1)
    s = jnp.where(ks <= qs, scores(q0), NEG)
    m0 = jnp.max(s, axis=0, keepdims=True)
    p = jnp.exp(s - m0)
    l0 = jnp.sum(p, axis=0, keepdims=True)
    acc0 = _dot(vt[:, pl.ds(q0, blk)], p.astype(BF16))

    def body(j, carry):
        m, l, acc = carry
        j0 = pl.multiple_of(j * blk, blk)
        s = scores(j0)
        if moba:
            s = s + selb[pl.ds(j, 1), :]
        m_new = jnp.maximum(m, jnp.max(s, axis=0, keepdims=True))
        alpha = jnp.exp(m - m_new)
        p = jnp.exp(s - m_new)
        l = alpha * l + jnp.sum(p, axis=0, keepdims=True)
        acc = alpha * acc + _dot(vt[:, pl.ds(j0, blk)], p.astype(BF16))
        return m_new, l, acc

    _, l, acc = lax.fori_loop(0, qi, body, (m0, l0, acc0))
    o_ref[...] = (acc / l).T.astype(BF16)


def _attn_prompt(pbf, k, v, cf, cf_t, *, n_seq, q_block, moba):
    bt = k.shape[0]
    t = bt // n_seq
    blk = MOBA_BLOCK
    nq = t // blk
    kv = pl.BlockSpec((t, HEAD_DIM), lambda b, h, i: (b, h))
    in_specs = [pl.BlockSpec((blk, HEAD_DIM), lambda b, h, i: (b * nq + i, q_block * N_HEADS + h)), kv, kv]
    args = [pbf, k, v]
    scratch = [pltpu.VMEM((t, HEAD_DIM), BF16), pltpu.VMEM((HEAD_DIM, t), BF16)]
    if moba:
        nbp = -(-nq // 8) * 8
        scratch += [pltpu.VMEM((nbp, HEAD_DIM), F32), pltpu.VMEM((nbp, blk), F32)]
    else:
        in_specs += [pl.BlockSpec((t, LANES), lambda b, h, i: (b, 0)), pl.BlockSpec((LANES, t), lambda b, h, i: (0, b))]
        args += [cf, cf_t]
        scratch += [pltpu.VMEM((t, LANES), F32)]
    return pl.pallas_call(
        functools.partial(_attn_kernel, moba=moba),
        grid=(n_seq, N_HEADS, nq),
        in_specs=in_specs,
        out_specs=pl.BlockSpec((blk, HEAD_DIM), lambda b, h, i: (b * nq + i, h)),
        out_shape=jax.ShapeDtypeStruct((bt, W_MIX), BF16),
        scratch_shapes=scratch,
        compiler_params=_cparams("parallel", "arbitrary", "arbitrary"),
        name="moba_prompt" if moba else "fox_prompt",
    )(*args)


def _mlstm_step_kernel(p_ref, gate_ref, bias_ref, conv_ref, wc_ref, hn_ref, c0_ref, n0_ref, m0_ref,
                       out_ref, c_ref, n_ref, m_ref):
    x = p_ref[...].astype(F32)
    qk = wc_ref[CONV_W - 1:CONV_W, :] * x[:, :2 * W_MIX]
    for j in range(CONV_W - 1):
        qk = qk + wc_ref[j:j + 1, :] * conv_ref[j:j + 1, :]
    qk = qk * _sigmoid(qk)
    q = qk[:, :W_MIX]
    k = qk[:, W_MIX:] * (HEAD_DIM ** -0.5)
    v = x[:, 2 * W_MIX:3 * W_MIX]
    og = _sigmoid(x[:, 3 * W_MIX:])
    z = gate_ref[...] + bias_ref[...]
    logf = _log_sigmoid(z)
    lane = lax.broadcasted_iota(jnp.int32, (1, LANES), 1)
    m_row = jnp.zeros((1, LANES), F32)
    outs = []
    for h in range(N_HEADS):
        hs = slice(h * HEAD_DIM, (h + 1) * HEAD_DIM)
        i_h = z[:, h:h + 1]
        f_h = logf[:, N_HEADS + h:N_HEADS + h + 1]
        m0 = m0_ref[:, h:h + 1]
        m_new = jnp.maximum(f_h + m0, i_h)
        a = jnp.exp(f_h + m0 - m_new)
        wgt = jnp.exp(i_h - m_new)
        k_col = jnp.broadcast_to(k[:, hs], (HEAD_DIM, HEAD_DIM)).T
        q_col = jnp.broadcast_to(q[:, hs], (HEAD_DIM, HEAD_DIM)).T
        c_new = a * c0_ref[h] + (wgt * k_col) * v[:, hs]
        n_new = a * n0_ref[h:h + 1, :] + wgt * k[:, hs]
        c_ref[h] = c_new
        n_ref[h:h + 1, :] = n_new
        num = jnp.sum(q_col * c_new, axis=0, keepdims=True)
        den = jnp.sum(q[:, hs] * n_new, axis=-1, keepdims=True)
        hh = num / jnp.maximum(jnp.abs(den), jnp.exp(-m_new))
        hh = hh * lax.rsqrt(jnp.mean(hh * hh, axis=-1, keepdims=True) + EPS) * hn_ref[:, hs]
        outs.append(og[:, hs] * hh)
        m_row = jnp.where(lane == h, m_new, m_row)
    out_ref[...] = jnp.concatenate(outs, axis=1).astype(BF16)
    m_ref[...] = m_row


def _mlstm_step(pbf3, gate3, bias, conv_prev, w_conv, g_hn, c0, n0, m0):
    bs = pbf3.shape[0]
    return pl.pallas_call(
        _mlstm_step_kernel,
        grid=(bs,),
        in_specs=[
            pl.BlockSpec((None, 1, 4 * W_MIX), lambda b: (b, 0, 0)),
            pl.BlockSpec((None, 1, LANES), lambda b: (b, 0, 0)),
            pl.BlockSpec((1, LANES), lambda b: (0, 0)),
            pl.BlockSpec((None, CONV_W - 1, 2 * W_MIX), lambda b: (b, 0, 0)),
            pl.BlockSpec((CONV_W, 2 * W_MIX), lambda b: (0, 0)),
            pl.BlockSpec((1, W_MIX), lambda b: (0, 0)),
            pl.BlockSpec((None, N_HEADS, HEAD_DIM, HEAD_DIM), lambda b: (b, 0, 0, 0)),
            pl.BlockSpec((None, N_HEADS, HEAD_DIM), lambda b: (b, 0, 0)),
            pl.BlockSpec((None, 1, N_HEADS), lambda b: (b, 0, 0)),
        ],
        out_specs=[
            pl.BlockSpec((None, 1, W_MIX), lambda b: (b, 0, 0)),
            pl.BlockSpec((None, N_HEADS, HEAD_DIM, HEAD_DIM), lambda b: (b, 0, 0, 0)),
            pl.BlockSpec((None, N_HEADS, HEAD_DIM), lambda b: (b, 0, 0)),
            pl.BlockSpec((None, 1, LANES), lambda b: (b, 0, 0)),
        ],
        out_shape=[
            jax.ShapeDtypeStruct((bs, 1, W_MIX), BF16),
            jax.ShapeDtypeStruct(c0.shape, F32),
            jax.ShapeDtypeStruct(n0.shape, F32),
            jax.ShapeDtypeStruct((bs, 1, LANES), F32),
        ],
        compiler_params=_cparams("parallel"),
        name="mlstm_step",
    )(pbf3, gate3, bias, conv_prev, w_conv, g_hn, c0, n0, m0)


def _head_rows(x_row):
    row = lax.broadcasted_iota(jnp.int32, (8, W_MIX), 0)
    lane = lax.broadcasted_iota(jnp.int32, (8, W_MIX), 1)
    return jnp.where(row == lane // HEAD_DIM, x_row, 0.0)


def _decode_kernel(pt_ref, *refs, group, n_pages):
    del pt_ref
    pages = refs[:4 * group]
    (fq_ref, bq_ref, knew_ref, vnew_ref, gate_ref, bias_ref,
     out_ref, lf_ref, top_ref, qf, qb, m_s, l_s, acc, carry, ksum) = refs[4 * group:]
    g = pl.program_id(1)
    n_blocks = n_pages // 2
    scale = HEAD_DIM ** -0.5

    @pl.when(g == 0)
    def _():
        qf[...] = _head_rows(fq_ref[...].astype(F32)).astype(BF16)
        qb[...] = _head_rows(bq_ref[...].astype(F32)).astype(BF16)
        m_s[...] = jnp.full(m_s.shape, -jnp.inf, F32)
        l_s[...] = jnp.zeros_like(l_s)
        acc[...] = jnp.zeros_like(acc)
        lf_new = _log_sigmoid(gate_ref[...] + bias_ref[...])
        lf_ref[...] = lf_new
        sub = lax.broadcasted_iota(jnp.int32, (N_HEADS, LANES), 0)
        lane = lax.broadcasted_iota(jnp.int32, (N_HEADS, LANES), 1)
        picked = jnp.where(lane == 2 * N_HEADS + sub, lf_new, 0.0)
        carry[...] = jnp.broadcast_to(jnp.sum(picked, axis=-1, keepdims=True), carry.shape)
        ksum[...] = jnp.zeros_like(ksum)

    r = lax.broadcasted_iota(jnp.int32, (PAGE, PAGE), 0)
    c = lax.broadcasted_iota(jnp.int32, (PAGE, PAGE), 1)
    later = jnp.where(r > c, 1.0, 0.0).astype(BF16)

    for i in range(group):
        fk_ref, fv_ref, bk_ref, lfp_ref = pages[4 * i:4 * i + 4]
        lfp = lfp_ref[...]
        bias = _dot3_l(lfp, later) + carry[...]
        carry[...] = carry[...] + jnp.sum(lfp, axis=-1, keepdims=True)
        s = _dot_nt(qf[...], fk_ref[...].astype(BF16))[:N_HEADS] * scale + bias
        m_old = m_s[...]
        m_new = jnp.maximum(m_old, jnp.max(s, axis=-1, keepdims=True))
        alpha = jnp.exp(m_old - m_new)
        p = jnp.exp(s - m_new)
        l_s[...] = alpha * l_s[...] + jnp.sum(p, axis=-1, keepdims=True)
        acc[...] = jnp.concatenate([alpha] * N_HEADS, axis=1) * acc[...] + _dot(p.astype(BF16), fv_ref[...].astype(BF16))
        m_s[...] = m_new
        page_sum = jnp.sum(bk_ref[...], axis=0, keepdims=True)
        blk = n_blocks - 1 - (g * group + i) // 2
        ksum[pl.ds(blk, 1), :] = ksum[pl.ds(blk, 1), :] + page_sum

    @pl.when(g == pl.num_programs(1) - 1)
    def _():
        qrows = qf[...].astype(F32)[:N_HEADS]
        s_new = jnp.sum(qrows * knew_ref[...], axis=-1, keepdims=True) * scale
        m_old = m_s[...]
        m_fin = jnp.maximum(m_old, s_new)
        alpha = jnp.exp(m_old - m_fin)
        p_new = jnp.exp(s_new - m_fin)
        l_fin = alpha * l_s[...] + p_new
        wide = lambda a: jnp.concatenate([a] * N_HEADS, axis=1)
        o = (wide(alpha) * acc[...] + wide(p_new) * vnew_ref[...]) / wide(l_fin)
        row = lax.broadcasted_iota(jnp.int32, (N_HEADS, W_MIX), 0)
        lane = lax.broadcasted_iota(jnp.int32, (N_HEADS, W_MIX), 1)
        out_ref[...] = jnp.sum(jnp.where(row == lane // HEAD_DIM, o, 0.0), axis=0, keepdims=True).astype(BF16)
        km_hi, km_mid, km_lo = _split3(ksum[...] * (1.0 / MOBA_BLOCK))
        gs = _dot_nt(qb[...], km_hi) + _dot_nt(qb[...], km_mid) + _dot_nt(qb[...], km_lo)
        blane = lax.broadcasted_iota(jnp.int32, gs.shape, 1).astype(F32)
        gs = jnp.where(blane < n_blocks, gs, -jnp.inf)
        top = jnp.zeros(gs.shape, F32)
        for slot in range(MOBA_TOPK):
            best = jnp.max(gs, axis=-1, keepdims=True)
            idx = jnp.min(jnp.where(gs == best, blane, float(LANES)), axis=-1, keepdims=True)
            top = jnp.where(blane == slot, idx, top)
            gs = jnp.where(blane == idx, -jnp.inf, gs)
        top_ref[...] = top.astype(jnp.int32)


def _decode_attn(page_table, cfk, cfv, cbk, lft, pbf3, fk3, fv3, gate3, bias, *, layer, group):
    bs, n_pages = page_table.shape
    assert n_pages % group == 0 and group % 2 == 0 and n_pages // 2 <= LANES and n_pages // 2 >= MOBA_TOPK

    def page_spec(i, shape):
        def imap(b, g, pt):
            return (layer, pt[b * n_pages + (n_pages - 1 - (g * group + i))], 0, 0)
        return pl.BlockSpec((None, None) + shape, imap)

    in_specs, args = [], []
    for i in range(group):
        in_specs += [page_spec(i, (PAGE, W_MIX)), page_spec(i, (PAGE, W_MIX)), page_spec(i, (PAGE, W_MIX)),
                     page_spec(i, (N_HEADS, PAGE))]
        args += [cfk, cfv, cbk, lft]
    row = lambda blk: pl.BlockSpec((None, 1, W_MIX), lambda b, g, pt: (b, 0, blk))
    in_specs += [row(4), row(5), row(0), row(0),
                 pl.BlockSpec((None, 1, LANES), lambda b, g, pt: (b, 0, 0)),
                 pl.BlockSpec((1, LANES), lambda b, g, pt: (0, 0))]
    args += [pbf3, pbf3, fk3, fv3, gate3, bias]
    return pl.pallas_call(
        functools.partial(_decode_kernel, group=group, n_pages=n_pages),
        grid_spec=pltpu.PrefetchScalarGridSpec(
            num_scalar_prefetch=1,
            grid=(bs, n_pages // group),
            in_specs=in_specs,
            out_specs=[
                pl.BlockSpec((None, 1, W_MIX), lambda b, g, pt: (b, 0, 0)),
                pl.BlockSpec((None, 1, LANES), lambda b, g, pt: (b, 0, 0)),
                pl.BlockSpec((None, 8, LANES), lambda b, g, pt: (b, 0, 0)),
            ],
            scratch_shapes=[
                pltpu.VMEM((8, W_MIX), BF16), pltpu.VMEM((8, W_MIX), BF16),
                pltpu.VMEM((N_HEADS, LANES), F32), pltpu.VMEM((N_HEADS, LANES), F32),
                pltpu.VMEM((N_HEADS, W_MIX), F32), pltpu.VMEM((N_HEADS, LANES), F32),
                pltpu.VMEM((LANES, W_MIX), F32),
            ],
        ),
        out_shape=[
            jax.ShapeDtypeStruct((bs, 1, W_MIX), BF16),
            jax.ShapeDtypeStruct((bs, 1, LANES), F32),
            jax.ShapeDtypeStruct((bs, 8, LANES), jnp.int32),
        ],
        compiler_params=_cparams("parallel", "arbitrary"),
        name="decode_attn",
    )(page_table.reshape(-1), *args)


def _moba_decode_kernel(pt_ref, top_ref, *refs):
    del pt_ref, top_ref
    n_pg = 2 * MOBA_TOPK
    kp, vp = refs[:n_pg], refs[n_pg:2 * n_pg]
    q_ref, knew_ref, vnew_ref, out_ref = refs[2 * n_pg:]
    scale = HEAD_DIM ** -0.5
    q8 = jnp.broadcast_to(q_ref[...], (8, HEAD_DIM))
    s = [_dot_nt(q8, k[...].astype(BF16))[0:1] * scale for k in kp]
    s_new = jnp.sum(q_ref[...].astype(F32) * knew_ref[...], axis=-1, keepdims=True) * scale
    m = s_new
    for si in s:
        m = jnp.maximum(m, jnp.max(si, axis=-1, keepdims=True))
    p_new = jnp.exp(s_new - m)
    l = p_new
    o = p_new * vnew_ref[...]
    for si, v in zip(s, vp):
        p = jnp.exp(si - m)
        l = l + jnp.sum(p, axis=-1, keepdims=True)
        o = o + _dot(jnp.broadcast_to(p, (8, PAGE)).astype(BF16), v[...].astype(BF16))[0:1]
    out_ref[...] = (o / l).astype(BF16)


def _moba_decode(page_table, top, cbk, cbv, pbf3, bk3, bv3, *, layer):
    bs, n_pages = page_table.shape

    def page_spec(slot, pg):
        def imap(b, h, pt, tp):
            return (layer, pt[b * n_pages + 2 * tp[(b * N_HEADS + h) * MOBA_TOPK + slot] + pg], 0, h)
        return pl.BlockSpec((None, None, PAGE, HEAD_DIM), imap)

    pages = [page_spec(s, g) for s in range(MOBA_TOPK) for g in range(2)]
    new = pl.BlockSpec((None, 1, HEAD_DIM), lambda b, h, pt, tp: (b, 0, h))
    return pl.pallas_call(
        _moba_decode_kernel,
        grid_spec=pltpu.PrefetchScalarGridSpec(
            num_scalar_prefetch=2,
            grid=(bs, N_HEADS),
            in_specs=pages + pages + [
                pl.BlockSpec((None, 1, HEAD_DIM), lambda b, h, pt, tp: (b, 0, 5 * N_HEADS + h)), new, new],
            out_specs=pl.BlockSpec((None, 1, HEAD_DIM), lambda b, h, pt, tp: (b, 0, h)),
        ),
        out_shape=jax.ShapeDtypeStruct((bs, 1, W_MIX), BF16),
        compiler_params=_cparams("parallel", "arbitrary"),
        name="moba_decode",
    )(page_table.reshape(-1), top.reshape(-1), *([cbk] * (2 * MOBA_TOPK)), *([cbv] * (2 * MOBA_TOPK)), pbf3, bk3, bv3)


ROW_TILE = 1024
MERGE_TILE = 512
FF_TILE = 256
MLSTM_CHUNK = 256
DECODE_GROUP = 8


def kernel(x_prompt, x_sample, cache_fox_k, cache_fox_v, cache_fox_logf, cache_moba_k, cache_moba_v,
           state_mlstm_C, state_mlstm_n, state_mlstm_m, state_mlstm_conv, page_table,
           norm_g, w_ffn1_gu, w_ffn1_d, w_ffn2_gu, w_ffn2_d, w_in, b_mlstm_i, b_mlstm_f, b_fox_f,
           w_conv, g_headnorm, w_br_m, w_br_f, w_br_b, w_out):
    n_seq, t, d = x_prompt.shape
    bs, dec_t, _ = x_sample.shape
    depth, n_pool = cache_fox_k.shape[:2]
    assert dec_t == 1 and t % MOBA_BLOCK == 0 and cache_fox_k.shape[2:] == (PAGE, N_HEADS, HEAD_DIM)
    assert page_table.shape[1] % DECODE_GROUP == 0
    chunk = math.gcd(t, MLSTM_CHUNK)
    tm = math.gcd(n_seq * t, ROW_TILE)
    tm_merge = math.gcd(n_seq * t, MERGE_TILE)
    tf = math.gcd(w_ffn1_d.shape[1], FF_TILE)

    w_all, w_gate = _prep_w_in(w_in)
    wgu1, wd1, wgu2, wd2 = (w.astype(BF16) for w in (w_ffn1_gu, w_ffn1_d, w_ffn2_gu, w_ffn2_d))
    wbm, wbf, wbb, wo = (w.astype(BF16) for w in (w_br_m, w_br_f, w_br_b, w_out))
    bias = jnp.pad(jnp.concatenate([b_mlstm_i, b_mlstm_f, b_fox_f], axis=-1),
                   ((0, 0), (0, LANES - 3 * N_HEADS)))[:, None, :]
    bias_t = jnp.swapaxes(bias, 1, 2)
    paged = lambda c: c.reshape(depth, n_pool, PAGE, W_MIX)
    cfk, cfv, cbk, cbv = paged(cache_fox_k), paged(cache_fox_v), paged(cache_moba_k), paged(cache_moba_v)
    lft = jnp.swapaxes(cache_fox_logf, 2, 3)
    hn = g_headnorm[:, None, :]

    xp = x_prompt.reshape(n_seq * t, d)
    xs = x_sample.reshape(bs, d)
    new_p, new_s = [], []
    heads = lambda a, g, tt: a.reshape(g, tt, N_HEADS, HEAD_DIM)
    for l in range(depth):
        g = [norm_g[l, i][None, :] for i in range(6)]
        xp = _ffn_block(xp, g[0], g[1], wgu1[l], wd1[l], tm=tm, tf=tf)
        pbf, fk, fv, bk, bv, gate, gate_t = _inproj(xp, g[2], w_all[l], w_gate[l], tm=tm, with_gate_t=True)
        act, cl, cf, act_t, cl_t, cf_t = _gateprep(gate, gate_t, bias[l], bias_t[l], n_seq=n_seq, chunk=chunk)
        m_out, caug, m_fin = _mlstm_prompt(pbf, w_conv[l], hn[l], act, cl, act_t, cl_t, n_seq=n_seq, chunk=chunk)
        f_out = _attn_prompt(pbf, fk, fv, cf, cf_t, n_seq=n_seq, q_block=4, moba=False)
        b_out = _attn_prompt(pbf, bk, bv, None, None, n_seq=n_seq, q_block=5, moba=True)
        xp = _merge_out(xp, m_out, f_out, b_out, pbf, g[3], wbm[l], wbf[l], wbb[l], wo[l], tm=tm_merge)
        xp = _ffn_block(xp, g[4], g[5], wgu2[l], wd2[l], tm=tm, tf=tf)
        conv_p = pbf.reshape(n_seq, t, -1)[:, t - (CONV_W - 1):, :2 * W_MIX].astype(F32)
        new_p.append((heads(fk, n_seq, t), heads(fv, n_seq, t), act[:, 2 * N_HEADS:3 * N_HEADS].reshape(n_seq, t, N_HEADS),
                      heads(bk, n_seq, t), heads(bv, n_seq, t),
                      caug[..., :HEAD_DIM], caug[..., HEAD_DIM], m_fin[:, :, 0, 0], conv_p))
        xs = _ffn_block(xs, g[0], g[1], wgu1[l], wd1[l], tm=bs, tf=tf)
        pbf_s, fk_s, fv_s, bk_s, bv_s, gate_s = _inproj(xs, g[2], w_all[l], w_gate[l], tm=bs, with_gate_t=False)
        row3 = lambda a: a.reshape(bs, 1, -1)
        pbf3, gate3 = row3(pbf_s), row3(gate_s)
        m_out_s, c_s, n_s, m_s = _mlstm_step(pbf3, gate3, bias[l], state_mlstm_conv[l], w_conv[l], hn[l],
                                             state_mlstm_C[l], state_mlstm_n[l], state_mlstm_m[l][:, None, :])
        f_out_s, lf_s, top = _decode_attn(page_table, cfk, cfv, cbk, lft, pbf3, row3(fk_s), row3(fv_s), gate3, bias[l],
                                          layer=l, group=DECODE_GROUP)
        b_out_s = _moba_decode(page_table, top[:, :N_HEADS, :MOBA_TOPK], cbk, cbv, pbf3, row3(bk_s), row3(bv_s), layer=l)
        flat = lambda a: a.reshape(bs, -1)
        xs = _merge_out(xs, flat(m_out_s), flat(f_out_s), flat(b_out_s), pbf_s, g[3], wbm[l], wbf[l], wbb[l], wo[l], tm=bs)
        xs = _ffn_block(xs, g[4], g[5], wgu2[l], wd2[l], tm=bs, tf=tf)
        conv_s = jnp.concatenate([state_mlstm_conv[l][:, 1:], pbf_s[:, None, :2 * W_MIX].astype(F32)], axis=1)
        new_s.append((heads(fk_s, bs, 1), heads(fv_s, bs, 1), lf_s[:, :, 2 * N_HEADS:3 * N_HEADS],
                      heads(bk_s, bs, 1), heads(bv_s, bs, 1), c_s, n_s, m_s[:, 0, :N_HEADS], conv_s))
    outs_p = [jnp.stack(z) for z in zip(*new_p)]
    outs_s = [jnp.stack(z) for z in zip(*new_s)]
    return (xp.reshape(n_seq, t, d), xs.reshape(bs, 1, d), *outs_p, *outs_s)
```

```python
import functools
import math

import jax
import jax.numpy as jnp
from jax import lax
from jax.experimental import pallas as pl
from jax.experimental.pallas import tpu as pltpu

F32 = jnp.float32
BF16 = jnp.bfloat16
EPS = 1e-6
NEG = -1e30

HEAD_DIM = 128
N_HEADS = 4
W_MIX = N_HEADS * HEAD_DIM
CONV_W = 4
MOBA_BLOCK = 256
MOBA_TOPK = 3
PAGE = 128
LANES = 128
VMEM_LIMIT = 52 * 1024 * 1024


def _cparams(*sem):
    return pltpu.CompilerParams(dimension_semantics=sem, vmem_limit_bytes=VMEM_LIMIT)


def _rms(x, g):
    return x * lax.rsqrt(jnp.mean(x * x, axis=-1, keepdims=True) + EPS) * g


def _split3(a):
    hi = a.astype(BF16)
    r = a - hi.astype(F32)
    mid = r.astype(BF16)
    lo = (r - mid.astype(F32)).astype(BF16)
    return hi, mid, lo


def _dot(a, b):
    return jnp.dot(a, b, preferred_element_type=F32)


def _dot_nt(a, b):
    return lax.dot_general(a, b, (((1,), (1,)), ((), ())), preferred_element_type=F32)


def _dot3_l(a_f32, b_bf16):
    hi, mid, lo = _split3(a_f32)
    return _dot(hi, b_bf16) + _dot(mid, b_bf16) + _dot(lo, b_bf16)


def _dot3_r(a_bf16, b_f32):
    hi, mid, lo = _split3(b_f32)
    return _dot(a_bf16, hi) + _dot(a_bf16, mid) + _dot(a_bf16, lo)


def _log_sigmoid(z):
    return jnp.minimum(z, 0.0) - jnp.log1p(jnp.exp(-jnp.abs(z)))


def _sigmoid(z):
    return 1.0 / (1.0 + jnp.exp(-z))


def _ffn_kernel(x_ref, ga_ref, gb_ref, wg_ref, wu_ref, wd_ref, o_ref, xn_ref, acc_ref):
    j = pl.program_id(1)

    @pl.when(j == 0)
    def _():
        xn_ref[...] = _rms(x_ref[...], ga_ref[...]).astype(BF16)
        acc_ref[...] = jnp.zeros_like(acc_ref)

    xn = xn_ref[...]
    g = _dot(xn, wg_ref[...])
    u = _dot(xn, wu_ref[...])
    h = (g * _sigmoid(g) * u).astype(BF16)
    acc_ref[...] += _dot(h, wd_ref[...])

    @pl.when(j == pl.num_programs(1) - 1)
    def _():
        o_ref[...] = x_ref[...] + 0.5 * _rms(acc_ref[...], gb_ref[...])


def _ffn_block(x, ga, gb, w_gu, w_d, *, tm, tf):
    m, d = x.shape
    f = w_d.shape[0]
    nj = f // tf
    return pl.pallas_call(
        _ffn_kernel,
        grid=(m // tm, nj),
        in_specs=[
            pl.BlockSpec((tm, d), lambda i, j: (i, 0)),
            pl.BlockSpec((1, d), lambda i, j: (0, 0)),
            pl.BlockSpec((1, d), lambda i, j: (0, 0)),
            pl.BlockSpec((d, tf), lambda i, j: (0, j)),
            pl.BlockSpec((d, tf), lambda i, j: (0, j + nj)),
            pl.BlockSpec((tf, d), lambda i, j: (j, 0)),
        ],
        out_specs=pl.BlockSpec((tm, d), lambda i, j: (i, 0)),
        out_shape=jax.ShapeDtypeStruct((m, d), F32),
        scratch_shapes=[pltpu.VMEM((tm, d), BF16), pltpu.VMEM((tm, d), F32)],
        compiler_params=_cparams("parallel", "arbitrary"),
        name="ffn_block",
    )(x, ga, gb, w_gu, w_gu, w_d)


N_BF_BLOCKS = 12
N_F32_BLOCKS = 4


def _inproj_kernel(x_ref, g_ref, w_ref, wgate_ref, pbf_ref, fk_ref, fv_ref, bk_ref, bv_ref,
                   gate_ref, *rest, with_gate_t):
    if with_gate_t:
        gate_t_ref, xn_ref = rest
    else:
        (xn_ref,) = rest
    j = pl.program_id(1)

    @pl.when(j == 0)
    def _():
        xn = _rms(x_ref[...], g_ref[...]).astype(BF16)
        xn_ref[...] = xn
        gate = _dot(xn, wgate_ref[...])
        gate_ref[...] = gate
        if with_gate_t:
            gate_t_ref[...] = gate.T

    @pl.when(j < N_BF_BLOCKS)
    def _():
        pbf_ref[...] = _dot(xn_ref[...], w_ref[...]).astype(BF16)

    for k, ref in enumerate((fk_ref, fv_ref, bk_ref, bv_ref)):
        @pl.when(j == N_BF_BLOCKS + k)
        def _(ref=ref):
            ref[...] = _dot(xn_ref[...], w_ref[...])


def _inproj(x, g, w_all, w_gate, *, tm, with_gate_t):
    m, d = x.shape
    nb = N_BF_BLOCKS + N_F32_BLOCKS
    wide = pl.BlockSpec((tm, W_MIX), lambda i, j: (i, 0))
    out_shape = [
        jax.ShapeDtypeStruct((m, N_BF_BLOCKS * W_MIX), BF16),
        jax.ShapeDtypeStruct((m, W_MIX), F32), jax.ShapeDtypeStruct((m, W_MIX), F32),
        jax.ShapeDtypeStruct((m, W_MIX), F32), jax.ShapeDtypeStruct((m, W_MIX), F32),
        jax.ShapeDtypeStruct((m, LANES), F32),
    ]
    out_specs = [
        pl.BlockSpec((tm, W_MIX), lambda i, j: (i, jnp.minimum(j, N_BF_BLOCKS - 1))),
        wide, wide, wide, wide,
        pl.BlockSpec((tm, LANES), lambda i, j: (i, 0)),
    ]
    if with_gate_t:
        out_shape.append(jax.ShapeDtypeStruct((LANES, m), F32))
        out_specs.append(pl.BlockSpec((LANES, tm), lambda i, j: (0, i)))
    return pl.pallas_call(
        functools.partial(_inproj_kernel, with_gate_t=with_gate_t),
        grid=(m // tm, nb),
        in_specs=[
            pl.BlockSpec((tm, d), lambda i, j: (i, 0)),
            pl.BlockSpec((1, d), lambda i, j: (0, 0)),
            pl.BlockSpec((d, W_MIX), lambda i, j: (0, j)),
            pl.BlockSpec((d, LANES), lambda i, j: (0, 0)),
        ],
        out_specs=out_specs,
        out_shape=out_shape,
        scratch_shapes=[pltpu.VMEM((tm, d), BF16)],
        compiler_params=_cparams("parallel", "arbitrary"),
        name="inproj",
    )(x, g, w_all, w_gate)


def _merge_kernel(x_ref, mo_ref, fo_ref, bo_ref, gm_ref, gf_ref, gb_ref, g_ref,
                  wm_ref, wf_ref, wb_ref, wo_ref, o_ref):
    merged = (_sigmoid(gm_ref[...].astype(F32)) * _dot(mo_ref[...], wm_ref[...])
              + _sigmoid(gf_ref[...].astype(F32)) * _dot(fo_ref[...], wf_ref[...])
              + _sigmoid(gb_ref[...].astype(F32)) * _dot(bo_ref[...], wb_ref[...]))
    y = _dot(merged.astype(BF16), wo_ref[...])
    o_ref[...] = x_ref[...] + _rms(y, g_ref[...])


def _merge_out(x, m_out, f_out, b_out, pbf, g, w_m, w_f, w_b, w_o, *, tm):
    m, d = x.shape
    gate_blk = d // W_MIX

    def gate_spec(k):
        return pl.BlockSpec((tm, d), lambda i: (i, (6 + k * gate_blk) // gate_blk))

    row = pl.BlockSpec((tm, W_MIX), lambda i: (i, 0))
    wbr = pl.BlockSpec((W_MIX, d), lambda i: (0, 0))
    return pl.pallas_call(
        _merge_kernel,
        grid=(m // tm,),
        in_specs=[
            pl.BlockSpec((tm, d), lambda i: (i, 0)), row, row, row,
            gate_spec(0), gate_spec(1), gate_spec(2),
            pl.BlockSpec((1, d), lambda i: (0, 0)),
            wbr, wbr, wbr, pl.BlockSpec((d, d), lambda i: (0, 0)),
        ],
        out_specs=pl.BlockSpec((tm, d), lambda i: (i, 0)),
        out_shape=jax.ShapeDtypeStruct((m, d), F32),
        compiler_params=_cparams("parallel"),
        name="merge_out",
    )(x, m_out, f_out, b_out, pbf, pbf, pbf, g, w_m, w_f, w_b, w_o)


def _w_in_offsets(d_model):
    widths = (("m_q", W_MIX), ("m_k", W_MIX), ("m_v", W_MIX), ("m_o", W_MIX), ("m_i", N_HEADS), ("m_f", N_HEADS),
              ("f_q", W_MIX), ("f_k", W_MIX), ("f_v", W_MIX), ("f_f", N_HEADS),
              ("b_q", W_MIX), ("b_k", W_MIX), ("b_v", W_MIX),
              ("g_m", d_model), ("g_f", d_model), ("g_b", d_model))
    out, off = {}, 0
    for name, w in widths:
        out[name] = (off, off + w)
        off += w
    return out


def _prep_w_in(w_in):
    d = w_in.shape[-2]
    offs = _w_in_offsets(d)

    def cols(name):
        a, b = offs[name]
        return w_in[..., a:b]

    order = ("m_q", "m_k", "m_v", "m_o", "f_q", "b_q", "g_m", "g_f", "g_b", "f_k", "f_v", "b_k", "b_v")
    w_all = jnp.concatenate([cols(n) for n in order], axis=-1).astype(BF16)
    gate = jnp.concatenate([cols("m_i"), cols("m_f"), cols("f_f")], axis=-1)
    pad = [(0, 0)] * (gate.ndim - 1) + [(0, LANES - gate.shape[-1])]
    w_gate = jnp.pad(gate, pad).astype(BF16)
    return w_all, w_gate


def _gateprep_kernel(gate_ref, gate_t_ref, bias_ref, bias_t_ref,
                     act_ref, cl_ref, act_t_ref, cl_t_ref, kb_ref, qb_ref, cf_ref, *, chunk):
    t = gate_ref.shape[0]
    lane = lax.broadcasted_iota(jnp.int32, (1, LANES), 1)
    z = gate_ref[...] + bias_ref[...]
    act_ref[...] = jnp.where(lane < N_HEADS, z, _log_sigmoid(z))
    sub = lax.broadcasted_iota(jnp.int32, (LANES, 1), 0)
    zt = gate_t_ref[...] + bias_t_ref[...]
    act_t_ref[...] = jnp.where(sub < N_HEADS, zt, _log_sigmoid(zt))

    r = lax.broadcasted_iota(jnp.int32, (chunk, chunk), 0)
    c = lax.broadcasted_iota(jnp.int32, (chunk, chunk), 1)
    tril = jnp.where(c <= r, 1.0, 0.0).astype(BF16)
    triu = jnp.where(r <= c, 1.0, 0.0).astype(BF16)
    carry = jnp.zeros((1, LANES), F32)
    for ci in range(t // chunk):
        sl = slice(ci * chunk, (ci + 1) * chunk)
        loc = _dot3_r(tril, act_ref[sl, :])
        cl_ref[sl, :] = loc
        full = loc + carry
        cf_ref[sl, :] = full
        carry = full[chunk - 1:chunk, :]
        cl_t_ref[:, sl] = _dot3_l(act_t_ref[:, sl], triu)

    terms = _split3(cf_ref[...] * (HEAD_DIM ** 0.5))
    r = lax.broadcasted_iota(jnp.int32, (LANES, W_MIX), 0)
    c = lax.broadcasted_iota(jnp.int32, (LANES, W_MIX), 1)
    own = r == 2 * N_HEADS + c // HEAD_DIM
    lane = lax.broadcasted_iota(jnp.int32, (1, W_MIX), 1) % HEAD_DIM

    def place(first):
        return sum(_dot(x, jnp.where(own & (c % HEAD_DIM == first + j), 1.0, 0.0).astype(BF16))
                   for j, x in enumerate(terms))

    kb_ref[...] = jnp.where((lane >= 3) & (lane < 6), 1.0, -place(0)).astype(BF16)
    qb_ref[...] = jnp.where(lane < 3, 1.0, place(3)).astype(BF16)


def _gateprep(gate, gate_t, bias, bias_t, *, n_seq, chunk):
    t = gate.shape[0] // n_seq
    col = pl.BlockSpec((t, LANES), lambda b: (b, 0))
    row = pl.BlockSpec((LANES, t), lambda b: (0, b))
    col_s = jax.ShapeDtypeStruct(gate.shape, F32)
    row_s = jax.ShapeDtypeStruct(gate_t.shape, F32)
    dec = pl.BlockSpec((t, W_MIX), lambda b: (b, 0))
    dec_s = jax.ShapeDtypeStruct((gate.shape[0], W_MIX), BF16)
    return pl.pallas_call(
        functools.partial(_gateprep_kernel, chunk=chunk),
        grid=(n_seq,),
        in_specs=[col, row, pl.BlockSpec((1, LANES), lambda b: (0, 0)), pl.BlockSpec((LANES, 1), lambda b: (0, 0))],
        out_specs=[col, col, row, row, dec, dec],
        out_shape=[col_s, col_s, row_s, row_s, dec_s, dec_s],
        scratch_shapes=[pltpu.VMEM((t, LANES), F32)],
        compiler_params=_cparams("parallel"),
        name="gateprep",
    )(gate, gate_t, bias, bias_t)


def _mlstm_kernel(q_ref, k_ref, v_ref, o_ref, wq_ref, wk_ref, hn_ref, act_ref, cl_ref, act_t_ref, cl_t_ref,
                  out_ref, caug_ref, m_ref, cbuf, qs, kts, icb, bcb, cst, *, chunk):
    h = pl.program_id(1)
    t = q_ref.shape[0]
    reps = chunk // LANES

    def wide(a, n):
        return jnp.concatenate([a] * n, axis=1) if n > 1 else a

    def conv_silu(x_ref, w_ref):
        cbuf[0:8, :] = jnp.zeros((8, LANES), F32)
        cbuf[8:, :] = x_ref[...].astype(F32)
        y = w_ref[0:1, :] * cbuf[pl.ds(8 - (CONV_W - 1), t), :]
        for j in range(1, CONV_W):
            y = y + w_ref[j:j + 1, :] * cbuf[pl.ds(8 - (CONV_W - 1) + j, t), :]
        return y * _sigmoid(y)

    qs[...] = conv_silu(q_ref, wq_ref).astype(BF16)
    kts[...] = (conv_silu(k_ref, wk_ref) * (HEAD_DIM ** -0.5)).T.astype(BF16)

    rr = lax.broadcasted_iota(jnp.int32, (LANES, LANES), 0)
    icb[...] = _dot3_l(act_ref[...], jnp.where(rr == h, 1.0, 0.0).astype(BF16))
    bcb[...] = _dot3_l(cl_ref[...], jnp.where(rr == N_HEADS + h, 1.0, 0.0).astype(BF16))

    cst[...] = jnp.zeros_like(cst)
    tt = lax.broadcasted_iota(jnp.int32, (chunk, chunk), 0)
    ss = lax.broadcasted_iota(jnp.int32, (chunk, chunk), 1)
    ones = jnp.ones((chunk, LANES), BF16)

    def body(c, m_prev):
        r0 = pl.multiple_of(c * chunk, chunk)
        rows = pl.ds(r0, chunk)
        qc = qs[rows, :]
        ktc = kts[:, rows]
        v_aug = jnp.concatenate([v_ref[rows, :], ones], axis=1)
        i_r = act_t_ref[pl.ds(h, 1), rows]
        b_r = cl_t_ref[pl.ds(N_HEADS + h, 1), rows]
        i_c = icb[rows, :]
        b_c = bcb[rows, :]
        dmat = jnp.where(ss <= tt, wide(b_c, reps) + (i_r - b_r), -jnp.inf)
        inter = b_c + m_prev
        m_t = jnp.maximum(inter, jnp.max(dmat, axis=-1, keepdims=True))
        w_inter = jnp.exp(inter - m_t)
        s = (_dot(qc, ktc) * jnp.exp(dmat - wide(m_t, reps))).astype(BF16)
        c_prev = cst[...]
        num = wide(w_inter, 2) * _dot(qc, c_prev.astype(BF16)) + _dot(s, v_aug)
        hh = num[:, :LANES] / jnp.maximum(jnp.abs(num[:, LANES:]), jnp.exp(-m_t))
        hh = hh * lax.rsqrt(jnp.mean(hh * hh, axis=-1, keepdims=True) + EPS) * hn_ref[...]
        out_ref[rows, :] = (_sigmoid(o_ref[rows, :].astype(F32)) * hh).astype(BF16)

        b_last = b_r[:, chunk - 1:chunk]
        m_new = jnp.maximum(b_last + m_prev, jnp.max(b_last - b_r + i_r, axis=-1, keepdims=True))
        a_prev = jnp.exp(b_last + m_prev - m_new)
        wg_c = jnp.exp(b_last - b_c + i_c - m_new)
        wv = (wide(wg_c, 2) * v_aug.astype(F32)).astype(BF16)
        cst[...] = a_prev * c_prev + _dot(ktc, wv)
        return m_new

    m_fin = lax.fori_loop(0, t // chunk, body, jnp.full((1, 1), -jnp.inf, F32))
    caug_ref[...] = cst[...]
    m_ref[...] = jnp.broadcast_to(m_fin, m_ref.shape)


def _mlstm_prompt(pbf, w_conv, g_hn, act, cl, act_t, cl_t, *, n_seq, chunk):
    bt = pbf.shape[0]
    t = bt // n_seq

    def head_cols(blk):
        return pl.BlockSpec((t, HEAD_DIM), lambda b, h: (b, blk * N_HEADS + h))

    col = pl.BlockSpec((t, LANES), lambda b, h: (b, 0))
    row = pl.BlockSpec((LANES, t), lambda b, h: (0, b))
    return pl.pallas_call(
        functools.partial(_mlstm_kernel, chunk=chunk),
        grid=(n_seq, N_HEADS),
        in_specs=[
            head_cols(0), head_cols(1), head_cols(2), head_cols(3),
            pl.BlockSpec((CONV_W, HEAD_DIM), lambda b, h: (0, h)),
            pl.BlockSpec((CONV_W, HEAD_DIM), lambda b, h: (0, N_HEADS + h)),
            pl.BlockSpec((1, HEAD_DIM), lambda b, h: (0, h)),
            col, col, row, row,
        ],
        out_specs=[
            pl.BlockSpec((t, HEAD_DIM), lambda b, h: (b, h)),
            pl.BlockSpec((None, None, HEAD_DIM, 2 * LANES), lambda b, h: (b, h, 0, 0)),
            pl.BlockSpec((None, None, 8, LANES), lambda b, h: (b, h, 0, 0)),
        ],
        out_shape=[
            jax.ShapeDtypeStruct((bt, W_MIX), BF16),
            jax.ShapeDtypeStruct((n_seq, N_HEADS, HEAD_DIM, 2 * LANES), F32),
            jax.ShapeDtypeStruct((n_seq, N_HEADS, 8, LANES), F32),
        ],
        scratch_shapes=[
            pltpu.VMEM((t + 8, LANES), F32), pltpu.VMEM((t, HEAD_DIM), BF16), pltpu.VMEM((HEAD_DIM, t), BF16),
            pltpu.VMEM((t, LANES), F32), pltpu.VMEM((t, LANES), F32), pltpu.VMEM((HEAD_DIM, 2 * LANES), F32),
        ],
        compiler_params=_cparams("parallel", "arbitrary"),
        name="mlstm_prompt",
    )(pbf, pbf, pbf, pbf, w_conv, w_conv, g_hn, act, cl, act_t, cl_t)


def _attn_kernel(*refs, moba):
    if moba:
        q_ref, k_ref, v_ref, o_ref, kaug, vt, qa0, pb0, qa1, pb1, kmean, sel0, sel1 = refs
        slots = ((qa0, pb0, sel0), (qa1, pb1, sel1))
    else:
        q_ref, k_ref, v_ref, kb_ref, qb_ref, o_ref, kaug, vt, qa0, pb0, qa1, pb1 = refs
        slots = ((qa0, pb0, None), (qa1, pb1, None))
    jp = pl.program_id(2)
    t = k_ref.shape[0]
    blk = qa0.shape[0]
    n_blk = t // blk
    scale = HEAD_DIM ** -0.5
    c_exp = scale * math.log2(math.e)

    @pl.when(jp == 0)
    def _():
        k = k_ref[...]
        kaug[:, :HEAD_DIM] = k.astype(BF16)
        vt[...] = v_ref[...].T.astype(BF16)
        if moba:
            lane = lax.broadcasted_iota(jnp.int32, (t, LANES), 1)
            kblk = lax.broadcasted_iota(jnp.int32, (t, LANES), 0) // blk
            kaug[:, HEAD_DIM:] = jnp.where(lane == kblk, 1.0, 0.0).astype(BF16)
            kmean[...] = jnp.zeros_like(kmean)
            for n in range(n_blk):
                kmean[n:n + 1, :] = jnp.mean(k[n * blk:(n + 1) * blk, :], axis=0, keepdims=True)
        else:
            kaug[:, HEAD_DIM:] = kb_ref[...]

    ks = lax.broadcasted_iota(jnp.int32, (blk, blk), 0)
    qs = lax.broadcasted_iota(jnp.int32, (blk, blk), 1)

    def attend(qi, slot):
        qaug, pbuf, selb = slots[slot]
        rows = slice(qi * blk, (qi + 1) * blk)
        qb = q_ref[rows, :]
        qaug[:, :HEAD_DIM] = qb
        if moba:
            nbp = kmean.shape[0]
            row = lax.broadcasted_iota(jnp.int32, (nbp, blk), 0)
            km_hi, km_mid, km_lo = _split3(kmean[...])
            gs = _dot_nt(km_hi, qb) + _dot_nt(km_mid, qb) + _dot_nt(km_lo, qb)
            gm = jnp.where(row < qi, gs, -jnp.inf)
            selb[...] = jnp.zeros_like(selb)
            for n in range(qi):
                g_n = gm[n:n + 1, :]
                beats = (gm > g_n) | ((gm == g_n) & (row < n))
                cnt = jnp.sum(jnp.where(beats, 1.0, 0.0), axis=0, keepdims=True)
                selb[n:n + 1, :] = jnp.where(cnt < MOBA_TOPK, 0.0, NEG / scale)
            qaug[:, HEAD_DIM:] = selb[...].T.astype(BF16)
        else:
            qaug[:, HEAD_DIM:] = qb_ref[rows, :]
        qa = qaug[...]

        def raw(c):
            s = _dot_nt(kaug[c * blk:(c + 1) * blk, :], qa)
            return jnp.where(ks <= qs, s, NEG) if c == qi else s

        m = jnp.max(raw(qi), axis=0, keepdims=True)
        for c in range(qi):
            m = jnp.maximum(m, jnp.max(raw(c), axis=0, keepdims=True))
        l = jnp.zeros((1, blk), F32)
        for c in range(qi + 1):
            p = jnp.exp2((raw(c) - m) * c_exp)
            l = l + jnp.sum(p, axis=0, keepdims=True)
            pbuf[c * blk:(c + 1) * blk, :] = p.astype(BF16)
        nk = (qi + 1) * blk
        acc = _dot(vt[:, 0:nk], pbuf[0:nk, :])
        o_ref[rows, :] = (acc / l).T.astype(BF16)

    def pair(j):
        attend(j, 0)
        if n_blk - 1 - j != j:
            attend(n_blk - 1 - j, 1)

    for j in range(-(-n_blk // 2)):
        pl.when(jp == j)(functools.partial(pair, j))


def _attn_prompt(pbf, k, v, kbias, qbias, *, n_seq, q_block, moba):
    bt = k.shape[0]
    t = bt // n_seq
    blk = MOBA_BLOCK
    nq = t // blk
    assert nq <= LANES
    head = pl.BlockSpec((t, HEAD_DIM), lambda b, h, j: (b, h))
    in_specs = [pl.BlockSpec((t, HEAD_DIM), lambda b, h, j: (b, q_block * N_HEADS + h)), head, head]
    args = [pbf, k, v]
    slot = [pltpu.VMEM((blk, 2 * HEAD_DIM), BF16), pltpu.VMEM((t, blk), BF16)]
    scratch = [pltpu.VMEM((t, 2 * HEAD_DIM), BF16), pltpu.VMEM((HEAD_DIM, t), BF16)] + slot + slot
    if moba:
        nbp = -(-nq // 8) * 8
        scratch += [pltpu.VMEM((nbp, HEAD_DIM), F32), pltpu.VMEM((LANES, blk), F32), pltpu.VMEM((LANES, blk), F32)]
    else:
        in_specs += [head, head]
        args += [kbias, qbias]
    return pl.pallas_call(
        functools.partial(_attn_kernel, moba=moba),
        grid=(n_seq, N_HEADS, -(-nq // 2)),
        in_specs=in_specs,
        out_specs=head,
        out_shape=jax.ShapeDtypeStruct((bt, W_MIX), BF16),
        scratch_shapes=scratch,
        compiler_params=_cparams("parallel", "arbitrary", "arbitrary"),
        name="moba_prompt" if moba else "fox_prompt",
    )(*args)


def _mlstm_step_kernel(p_ref, gate_ref, bias_ref, conv_ref, wc_ref, hn_ref, c0_ref, n0_ref, m0_ref,
                       out_ref, c_ref, n_ref, m_ref):
    x = p_ref[...].astype(F32)
    qk = wc_ref[CONV_W - 1:CONV_W, :] * x[:, :2 * W_MIX]
    for j in range(CONV_W - 1):
        qk = qk + wc_ref[j:j + 1, :] * conv_ref[j:j + 1, :]
    qk = qk * _sigmoid(qk)
    q = qk[:, :W_MIX]
    k = qk[:, W_MIX:] * (HEAD_DIM ** -0.5)
    v = x[:, 2 * W_MIX:3 * W_MIX]
    og = _sigmoid(x[:, 3 * W_MIX:])
    z = gate_ref[...] + bias_ref[...]
    logf = _log_sigmoid(z)
    lane = lax.broadcasted_iota(jnp.int32, (1, LANES), 1)
    m_row = jnp.zeros((1, LANES), F32)
    outs = []
    for h in range(N_HEADS):
        hs = slice(h * HEAD_DIM, (h + 1) * HEAD_DIM)
        i_h = z[:, h:h + 1]
        f_h = logf[:, N_HEADS + h:N_HEADS + h + 1]
        m0 = m0_ref[:, h:h + 1]
        m_new = jnp.maximum(f_h + m0, i_h)
        a = jnp.exp(f_h + m0 - m_new)
        wgt = jnp.exp(i_h - m_new)
        k_col = jnp.broadcast_to(k[:, hs], (HEAD_DIM, HEAD_DIM)).T
        q_col = jnp.broadcast_to(q[:, hs], (HEAD_DIM, HEAD_DIM)).T
        c_new = a * c0_ref[h] + (wgt * k_col) * v[:, hs]
        n_new = a * n0_ref[h:h + 1, :] + wgt * k[:, hs]
        c_ref[h] = c_new
        n_ref[h:h + 1, :] = n_new
        num = jnp.sum(q_col * c_new, axis=0, keepdims=True)
        den = jnp.sum(q[:, hs] * n_new, axis=-1, keepdims=True)
        hh = num / jnp.maximum(jnp.abs(den), jnp.exp(-m_new))
        hh = hh * lax.rsqrt(jnp.mean(hh * hh, axis=-1, keepdims=True) + EPS) * hn_ref[:, hs]
        outs.append(og[:, hs] * hh)
        m_row = jnp.where(lane == h, m_new, m_row)
    out_ref[...] = jnp.concatenate(outs, axis=1).astype(BF16)
    m_ref[...] = m_row


def _mlstm_step(pbf3, gate3, bias, conv_prev, w_conv, g_hn, c0, n0, m0):
    bs = pbf3.shape[0]
    return pl.pallas_call(
        _mlstm_step_kernel,
        grid=(bs,),
        in_specs=[
            pl.BlockSpec((None, 1, 4 * W_MIX), lambda b: (b, 0, 0)),
            pl.BlockSpec((None, 1, LANES), lambda b: (b, 0, 0)),
            pl.BlockSpec((1, LANES), lambda b: (0, 0)),
            pl.BlockSpec((None, CONV_W - 1, 2 * W_MIX), lambda b: (b, 0, 0)),
            pl.BlockSpec((CONV_W, 2 * W_MIX), lambda b: (0, 0)),
            pl.BlockSpec((1, W_MIX), lambda b: (0, 0)),
            pl.BlockSpec((None, N_HEADS, HEAD_DIM, HEAD_DIM), lambda b: (b, 0, 0, 0)),
            pl.BlockSpec((None, N_HEADS, HEAD_DIM), lambda b: (b, 0, 0)),
            pl.BlockSpec((None, 1, N_HEADS), lambda b: (b, 0, 0)),
        ],
        out_specs=[
            pl.BlockSpec((None, 1, W_MIX), lambda b: (b, 0, 0)),
            pl.BlockSpec((None, N_HEADS, HEAD_DIM, HEAD_DIM), lambda b: (b, 0, 0, 0)),
            pl.BlockSpec((None, N_HEADS, HEAD_DIM), lambda b: (b, 0, 0)),
            pl.BlockSpec((None, 1, LANES), lambda b: (b, 0, 0)),
        ],
        out_shape=[
            jax.ShapeDtypeStruct((bs, 1, W_MIX), BF16),
            jax.ShapeDtypeStruct(c0.shape, F32),
            jax.ShapeDtypeStruct(n0.shape, F32),
            jax.ShapeDtypeStruct((bs, 1, LANES), F32),
        ],
        compiler_params=_cparams("parallel"),
        name="mlstm_step",
    )(pbf3, gate3, bias, conv_prev, w_conv, g_hn, c0, n0, m0)


PAGE_ROWS = PAGE * N_HEADS


def _pad_rows8(a):
    return jnp.concatenate([a, jnp.zeros_like(a)], axis=0)


def _decode_kernel(pt_ref, *refs, group, n_pages):
    del pt_ref
    pages = refs[:4 * group]
    (fq_ref, bq_ref, knew_ref, vnew_ref, gate_ref, bias_ref,
     out_ref, lf_ref, top_ref, qf, qb, m_s, l_s, acc, carry, ksum, msuf, mtot) = refs[4 * group:]
    first = (pl.program_id(0) == 0) & (pl.program_id(1) == 0)
    g = pl.program_id(1)
    n_blocks = n_pages // 2
    scale = HEAD_DIM ** -0.5

    @pl.when(first)
    def _():
        src = lax.broadcasted_iota(jnp.int32, (PAGE_ROWS, PAGE_ROWS), 0)
        dst = lax.broadcasted_iota(jnp.int32, (PAGE_ROWS, PAGE_ROWS), 1)
        same_head = (src & (N_HEADS - 1)) == (dst & (N_HEADS - 1))
        mtot[...] = jnp.where(same_head, 1.0, 0.0).astype(BF16)
        msuf[...] = jnp.where(same_head & (src > dst), 1.0, 0.0).astype(BF16)

    @pl.when(g == 0)
    def _():
        qf[...] = _pad_rows8(fq_ref[...].astype(F32))
        qb[...] = _pad_rows8(bq_ref[...].astype(F32))
        m_s[...] = jnp.full(m_s.shape, NEG, F32)
        l_s[...] = jnp.zeros_like(l_s)
        acc[...] = jnp.zeros_like(acc)
        lf_new = _log_sigmoid(gate_ref[...] + bias_ref[...])
        lf_ref[...] = lf_new
        r = lax.broadcasted_iota(jnp.int32, (LANES, PAGE_ROWS), 0)
        c = lax.broadcasted_iota(jnp.int32, (LANES, PAGE_ROWS), 1)
        spread = jnp.where(r == 2 * N_HEADS + (c & (N_HEADS - 1)), 1.0, 0.0).astype(BF16)
        carry[...] = _dot3_l(jnp.broadcast_to(lf_new, (8, LANES)), spread)[0:1]
        ksum[...] = jnp.zeros_like(ksum)

    row = lax.broadcasted_iota(jnp.int32, (8, PAGE_ROWS), 0)
    lane = lax.broadcasted_iota(jnp.int32, (8, PAGE_ROWS), 1)
    own_head = (lane & (N_HEADS - 1)) == row
    wide = lambda a: jnp.concatenate([a] * N_HEADS, axis=1)

    lf_rows = jnp.concatenate([pages[4 * i + 3][...] for i in range(group)], axis=0)
    suf = _dot3_l(lf_rows, msuf[...])
    tot = _dot3_l(lf_rows, mtot[...])
    run = carry[...]
    q_f = qf[...].astype(BF16)
    s_all = []
    for i in range(group):
        s = _dot_nt(q_f, pages[4 * i][...].astype(BF16)) * scale + (suf[i:i + 1] + run)
        s_all.append(jnp.where(own_head, s, NEG))
        run = run + tot[i:i + 1]
    carry[...] = run
    m_old = m_s[...]
    m_new = m_old
    for s in s_all:
        m_new = jnp.maximum(m_new, jnp.max(s, axis=-1, keepdims=True))
    alpha = jnp.exp(m_old - m_new)
    l_new = alpha * l_s[...]
    a_new = alpha * acc[...]
    m_wide = wide(m_new)
    for i, s in enumerate(s_all):
        p = jnp.exp(s - m_wide)
        l_new = l_new + jnp.sum(p, axis=-1, keepdims=True)
        a_new = a_new + _dot(p.astype(BF16), pages[4 * i + 1][...].astype(BF16))
    m_s[...] = m_new
    l_s[...] = l_new
    acc[...] = a_new

    sub8 = lax.broadcasted_iota(jnp.int32, (8, HEAD_DIM), 0)
    for i in range(group):
        vs = jnp.sum(pages[4 * i + 2][...].reshape(PAGE_ROWS // 8, 8, HEAD_DIM), axis=0)
        per_head = jnp.where(sub8 < N_HEADS, vs + pltpu.roll(vs, N_HEADS, 0), 0.0)
        r0 = pl.multiple_of((n_blocks - 1 - (g * group + i) // 2) * 8, 8)
        ksum[pl.ds(r0, 8), :] = ksum[pl.ds(r0, 8), :] + per_head

    @pl.when(g == pl.num_programs(1) - 1)
    def _():
        s_new = jnp.sum(qf[...] * _pad_rows8(knew_ref[...]), axis=-1, keepdims=True) * scale
        m_old = m_s[...]
        m_fin = jnp.maximum(m_old, s_new)
        alpha = jnp.exp(m_old - m_fin)
        p_new = jnp.exp(s_new - m_fin)
        l_fin = alpha * l_s[...] + p_new
        o = (alpha * acc[...] + p_new * _pad_rows8(vnew_ref[...])) / l_fin
        out_ref[...] = o[0:N_HEADS].astype(BF16)
        km_hi, km_mid, km_lo = _split3(ksum[...] * (1.0 / MOBA_BLOCK))
        q_b = qb[...].astype(BF16)
        gs = _dot_nt(q_b, km_hi) + _dot_nt(q_b, km_mid) + _dot_nt(q_b, km_lo)
        grow = lax.broadcasted_iota(jnp.int32, gs.shape, 0)
        glane = lax.broadcasted_iota(jnp.int32, gs.shape, 1)
        gs = jnp.where((glane & 7) == grow, gs, -jnp.inf)
        glane_f = glane.astype(F32)
        tlane = lax.broadcasted_iota(jnp.int32, (8, LANES), 1)
        top = jnp.zeros((8, LANES), F32)
        for slot in range(MOBA_TOPK):
            best = jnp.max(gs, axis=-1, keepdims=True)
            idx = jnp.min(jnp.where(gs == best, glane_f, float(8 * n_blocks)), axis=-1, keepdims=True)
            top = jnp.where(tlane == slot, idx, top)
            gs = jnp.where(glane_f == idx, -jnp.inf, gs)
        top_ref[...] = jnp.right_shift(top.astype(jnp.int32), 3)


def _decode_attn(page_table, cfk, cfv, cbk, lfc, fq4, bq4, fk4, fv4, gate3, bias, *, layer, group):
    bs, n_pages = page_table.shape
    assert n_pages % group == 0 and group % 2 == 0 and n_pages // 2 >= MOBA_TOPK

    def page_spec(i, shape):
        def imap(b, g, pt):
            return (layer, pt[b * n_pages + (n_pages - 1 - (g * group + i))], 0, 0)
        return pl.BlockSpec((None, None) + shape, imap)

    in_specs, args = [], []
    for i in range(group):
        in_specs += [page_spec(i, (PAGE_ROWS, HEAD_DIM)), page_spec(i, (PAGE_ROWS, HEAD_DIM)),
                     page_spec(i, (PAGE_ROWS, HEAD_DIM)), page_spec(i, (1, PAGE_ROWS))]
        args += [cfk, cfv, cbk, lfc]
    per_head = pl.BlockSpec((None, N_HEADS, HEAD_DIM), lambda b, g, pt: (b, 0, 0))
    in_specs += [per_head, per_head, per_head, per_head,
                 pl.BlockSpec((None, 1, LANES), lambda b, g, pt: (b, 0, 0)),
                 pl.BlockSpec((1, LANES), lambda b, g, pt: (0, 0))]
    args += [fq4, bq4, fk4, fv4, gate3, bias]
    small = pltpu.VMEM((8, LANES), F32)
    return pl.pallas_call(
        functools.partial(_decode_kernel, group=group, n_pages=n_pages),
        grid_spec=pltpu.PrefetchScalarGridSpec(
            num_scalar_prefetch=1,
            grid=(bs, n_pages // group),
            in_specs=in_specs,
            out_specs=[
                per_head,
                pl.BlockSpec((None, 1, LANES), lambda b, g, pt: (b, 0, 0)),
                pl.BlockSpec((None, 8, LANES), lambda b, g, pt: (b, 0, 0)),
            ],
            scratch_shapes=[
                small, small, small, small, small,
                pltpu.VMEM((1, PAGE_ROWS), F32),
                pltpu.VMEM((8 * (n_pages // 2), HEAD_DIM), F32),
                pltpu.VMEM((PAGE_ROWS, PAGE_ROWS), BF16), pltpu.VMEM((PAGE_ROWS, PAGE_ROWS), BF16),
            ],
        ),
        out_shape=[
            jax.ShapeDtypeStruct((bs, N_HEADS, HEAD_DIM), BF16),
            jax.ShapeDtypeStruct((bs, 1, LANES), F32),
            jax.ShapeDtypeStruct((bs, 8, LANES), jnp.int32),
        ],
        compiler_params=_cparams("arbitrary", "arbitrary"),
        name="decode_attn",
    )(page_table.reshape(-1), *args)


def _moba_decode_kernel(pt_ref, top_ref, *refs):
    del pt_ref, top_ref
    n_pg = 2 * MOBA_TOPK
    kp, vp = refs[:n_pg], refs[n_pg:2 * n_pg]
    q_ref, knew_ref, vnew_ref, out_ref = refs[2 * n_pg:]
    h = pl.program_id(1)
    scale = HEAD_DIM ** -0.5
    q8 = jnp.broadcast_to(q_ref[...], (8, HEAD_DIM))
    lane = lax.broadcasted_iota(jnp.int32, (1, PAGE_ROWS), 1)
    own_head = (lane & (N_HEADS - 1)) == h
    s = [jnp.where(own_head, _dot_nt(q8, k[...].astype(BF16))[0:1] * scale, NEG) for k in kp]
    s_new = jnp.sum(q_ref[...].astype(F32) * knew_ref[...], axis=-1, keepdims=True) * scale
    m = s_new
    for si in s:
        m = jnp.maximum(m, jnp.max(si, axis=-1, keepdims=True))
    p_new = jnp.exp(s_new - m)
    l = p_new
    o = p_new * vnew_ref[...]
    for si, v in zip(s, vp):
        p = jnp.exp(si - m)
        l = l + jnp.sum(p, axis=-1, keepdims=True)
        o = o + _dot(jnp.broadcast_to(p, (8, PAGE_ROWS)).astype(BF16), v[...].astype(BF16))[0:1]
    out_ref[...] = (o / l).astype(BF16)


def _moba_decode(page_table, top, cbk, cbv, bq4, bk4, bv4, *, layer):
    bs, n_pages = page_table.shape

    def page_spec(slot, pg):
        def imap(b, h, pt, tp):
            return (layer, pt[b * n_pages + 2 * tp[(b * N_HEADS + h) * MOBA_TOPK + slot] + pg], 0, 0)
        return pl.BlockSpec((None, None, PAGE_ROWS, HEAD_DIM), imap)

    pages = [page_spec(s, g) for s in range(MOBA_TOPK) for g in range(2)]
    one = pl.BlockSpec((None, None, 1, HEAD_DIM), lambda b, h, pt, tp: (b, h, 0, 0))
    return pl.pallas_call(
        _moba_decode_kernel,
        grid_spec=pltpu.PrefetchScalarGridSpec(
            num_scalar_prefetch=2,
            grid=(bs, N_HEADS),
            in_specs=pages + pages + [one, one, one],
            out_specs=one,
        ),
        out_shape=jax.ShapeDtypeStruct((bs, N_HEADS, 1, HEAD_DIM), BF16),
        compiler_params=_cparams("parallel", "arbitrary"),
        name="moba_decode",
    )(page_table.reshape(-1), top.reshape(-1), *([cbk] * (2 * MOBA_TOPK)), *([cbv] * (2 * MOBA_TOPK)), bq4, bk4, bv4)


ROW_TILE = 1024
MERGE_TILE = 512
FF_TILE = 256
MLSTM_CHUNK = 256
DECODE_GROUP = 8


def kernel(x_prompt, x_sample, cache_fox_k, cache_fox_v, cache_fox_logf, cache_moba_k, cache_moba_v,
           state_mlstm_C, state_mlstm_n, state_mlstm_m, state_mlstm_conv, page_table,
           norm_g, w_ffn1_gu, w_ffn1_d, w_ffn2_gu, w_ffn2_d, w_in, b_mlstm_i, b_mlstm_f, b_fox_f,
           w_conv, g_headnorm, w_br_m, w_br_f, w_br_b, w_out):
    n_seq, t, d = x_prompt.shape
    bs, dec_t, _ = x_sample.shape
    depth, n_pool = cache_fox_k.shape[:2]
    assert dec_t == 1 and t % MOBA_BLOCK == 0 and cache_fox_k.shape[2:] == (PAGE, N_HEADS, HEAD_DIM)
    assert page_table.shape[1] % DECODE_GROUP == 0
    chunk = math.gcd(t, MLSTM_CHUNK)
    tm = math.gcd(n_seq * t, ROW_TILE)
    tm_merge = math.gcd(n_seq * t, MERGE_TILE)
    tf = math.gcd(w_ffn1_d.shape[1], FF_TILE)

    w_all, w_gate = _prep_w_in(w_in)
    wgu1, wd1, wgu2, wd2 = (w.astype(BF16) for w in (w_ffn1_gu, w_ffn1_d, w_ffn2_gu, w_ffn2_d))
    wbm, wbf, wbb, wo = (w.astype(BF16) for w in (w_br_m, w_br_f, w_br_b, w_out))
    bias = jnp.pad(jnp.concatenate([b_mlstm_i, b_mlstm_f, b_fox_f], axis=-1),
                   ((0, 0), (0, LANES - 3 * N_HEADS)))[:, None, :]
    bias_t = jnp.swapaxes(bias, 1, 2)
    paged = lambda c: c.reshape(depth, n_pool, PAGE_ROWS, HEAD_DIM)
    cfk, cfv, cbk, cbv = paged(cache_fox_k), paged(cache_fox_v), paged(cache_moba_k), paged(cache_moba_v)
    lfc = cache_fox_logf.reshape(depth, n_pool, 1, PAGE_ROWS)
    hn = g_headnorm[:, None, :]

    xp = x_prompt.reshape(n_seq * t, d)
    xs = x_sample.reshape(bs, d)
    new_p, new_s = [], []
    heads = lambda a, g, tt: a.reshape(g, tt, N_HEADS, HEAD_DIM)
    for l in range(depth):
        g = [norm_g[l, i][None, :] for i in range(6)]
        xp = _ffn_block(xp, g[0], g[1], wgu1[l], wd1[l], tm=tm, tf=tf)
        pbf, fk, fv, bk, bv, gate, gate_t = _inproj(xp, g[2], w_all[l], w_gate[l], tm=tm, with_gate_t=True)
        act, cl, act_t, cl_t, kbias, qbias = _gateprep(gate, gate_t, bias[l], bias_t[l], n_seq=n_seq, chunk=chunk)
        m_out, caug, m_fin = _mlstm_prompt(pbf, w_conv[l], hn[l], act, cl, act_t, cl_t, n_seq=n_seq, chunk=chunk)
        f_out = _attn_prompt(pbf, fk, fv, kbias, qbias, n_seq=n_seq, q_block=4, moba=False)
        b_out = _attn_prompt(pbf, bk, bv, None, None, n_seq=n_seq, q_block=5, moba=True)
        xp = _merge_out(xp, m_out, f_out, b_out, pbf, g[3], wbm[l], wbf[l], wbb[l], wo[l], tm=tm_merge)
        xp = _ffn_block(xp, g[4], g[5], wgu2[l], wd2[l], tm=tm, tf=tf)
        conv_p = pbf.reshape(n_seq, t, -1)[:, t - (CONV_W - 1):, :2 * W_MIX].astype(F32)
        new_p.append((heads(fk, n_seq, t), heads(fv, n_seq, t), act[:, 2 * N_HEADS:3 * N_HEADS].reshape(n_seq, t, N_HEADS),
                      heads(bk, n_seq, t), heads(bv, n_seq, t),
                      caug[..., :HEAD_DIM], caug[..., HEAD_DIM], m_fin[:, :, 0, 0], conv_p))
        xs = _ffn_block(xs, g[0], g[1], wgu1[l], wd1[l], tm=bs, tf=tf)
        pbf_s, fk_s, fv_s, bk_s, bv_s, gate_s = _inproj(xs, g[2], w_all[l], w_gate[l], tm=bs, with_gate_t=False)
        row3 = lambda a: a.reshape(bs, 1, -1)
        pbf3, gate3 = row3(pbf_s), row3(gate_s)
        m_out_s, c_s, n_s, m_s = _mlstm_step(pbf3, gate3, bias[l], state_mlstm_conv[l], w_conv[l], hn[l],
                                             state_mlstm_C[l], state_mlstm_n[l], state_mlstm_m[l][:, None, :])
        h3 = lambda a: a.reshape(bs, N_HEADS, HEAD_DIM)
        h4 = lambda a: a.reshape(bs, N_HEADS, 1, HEAD_DIM)
        fq_s, bq_s = pbf_s[:, 4 * W_MIX:5 * W_MIX], pbf_s[:, 5 * W_MIX:6 * W_MIX]
        f_out_s, lf_s, top = _decode_attn(page_table, cfk, cfv, cbk, lfc, h3(fq_s), h3(bq_s), h3(fk_s), h3(fv_s),
                                          gate3, bias[l], layer=l, group=DECODE_GROUP)
        b_out_s = _moba_decode(page_table, top[:, :N_HEADS, :MOBA_TOPK], cbk, cbv, h4(bq_s), h4(bk_s), h4(bv_s), layer=l)
        flat = lambda a: a.reshape(bs, -1)
        xs = _merge_out(xs, flat(m_out_s), flat(f_out_s), flat(b_out_s), pbf_s, g[3], wbm[l], wbf[l], wbb[l], wo[l], tm=bs)
        xs = _ffn_block(xs, g[4], g[5], wgu2[l], wd2[l], tm=bs, tf=tf)
        conv_s = jnp.concatenate([state_mlstm_conv[l][:, 1:], pbf_s[:, None, :2 * W_MIX].astype(F32)], axis=1)
        new_s.append((heads(fk_s, bs, 1), heads(fv_s, bs, 1), lf_s[:, :, 2 * N_HEADS:3 * N_HEADS],
                      heads(bk_s, bs, 1), heads(bv_s, bs, 1), c_s, n_s, m_s[:, 0, :N_HEADS], conv_s))
    outs_p = [jnp.stack(z) for z in zip(*new_p)]
    outs_s = [jnp.stack(z) for z in zip(*new_s)]
    return (xp.reshape(n_seq, t, d), xs.reshape(bs, 1, d), *outs_p, *outs_s)
```

```python
import functools
import math

import jax
import jax.numpy as jnp
from jax import lax
from jax.experimental import pallas as pl
from jax.experimental.pallas import tpu as pltpu

F32 = jnp.float32
BF16 = jnp.bfloat16
EPS = 1e-6
NEG = -1e30

HEAD_DIM = 128
N_HEADS = 4
W_MIX = N_HEADS * HEAD_DIM
CONV_W = 4
MOBA_BLOCK = 256
MOBA_TOPK = 3
PAGE = 128
LANES = 128
VMEM_LIMIT = 52 * 1024 * 1024


def _cparams(*sem):
    return pltpu.CompilerParams(dimension_semantics=sem, vmem_limit_bytes=VMEM_LIMIT)


def _rms(x, g):
    return x * lax.rsqrt(jnp.mean(x * x, axis=-1, keepdims=True) + EPS) * g


def _split3(a):
    hi = a.astype(BF16)
    r = a - hi.astype(F32)
    mid = r.astype(BF16)
    lo = (r - mid.astype(F32)).astype(BF16)
    return hi, mid, lo


def _dot(a, b):
    return jnp.dot(a, b, preferred_element_type=F32)


def _dot_nt(a, b):
    return lax.dot_general(a, b, (((1,), (1,)), ((), ())), preferred_element_type=F32)


def _dot3_l(a_f32, b_bf16):
    hi, mid, lo = _split3(a_f32)
    return _dot(hi, b_bf16) + _dot(mid, b_bf16) + _dot(lo, b_bf16)


def _dot3_r(a_bf16, b_f32):
    hi, mid, lo = _split3(b_f32)
    return _dot(a_bf16, hi) + _dot(a_bf16, mid) + _dot(a_bf16, lo)


def _log_sigmoid(z):
    return jnp.minimum(z, 0.0) - jnp.log1p(jnp.exp(-jnp.abs(z)))


def _sigmoid(z):
    return 1.0 / (1.0 + jnp.exp(-z))


def _ffn_kernel(x_ref, ga_ref, gb_ref, wg_ref, wu_ref, wd_ref, o_ref, xn_ref, acc_ref):
    j = pl.program_id(1)

    @pl.when(j == 0)
    def _():
        xn_ref[...] = _rms(x_ref[...], ga_ref[...]).astype(BF16)
        acc_ref[...] = jnp.zeros_like(acc_ref)

    xn = xn_ref[...]
    g = _dot(xn, wg_ref[...])
    u = _dot(xn, wu_ref[...])
    h = (g * _sigmoid(g) * u).astype(BF16)
    acc_ref[...] += _dot(h, wd_ref[...])

    @pl.when(j == pl.num_programs(1) - 1)
    def _():
        o_ref[...] = x_ref[...] + 0.5 * _rms(acc_ref[...], gb_ref[...])


def _ffn_block(x, ga, gb, w_gu, w_d, *, layer, tm, tf):
    m, d = x.shape
    f = w_d.shape[1]
    nj = f // tf
    return pl.pallas_call(
        _ffn_kernel,
        grid=(m // tm, nj),
        in_specs=[
            pl.BlockSpec((tm, d), lambda i, j: (i, 0)),
            pl.BlockSpec((1, d), lambda i, j: (0, 0)),
            pl.BlockSpec((1, d), lambda i, j: (0, 0)),
            pl.BlockSpec((None, d, tf), lambda i, j: (layer, 0, j)),
            pl.BlockSpec((None, d, tf), lambda i, j: (layer, 0, j + nj)),
            pl.BlockSpec((None, tf, d), lambda i, j: (layer, j, 0)),
        ],
        out_specs=pl.BlockSpec((tm, d), lambda i, j: (i, 0)),
        out_shape=jax.ShapeDtypeStruct((m, d), F32),
        scratch_shapes=[pltpu.VMEM((tm, d), BF16), pltpu.VMEM((tm, d), F32)],
        compiler_params=_cparams("parallel", "arbitrary"),
        name="ffn_block",
    )(x, ga, gb, w_gu, w_gu, w_d)


N_BF_BLOCKS = 12
N_F32_BLOCKS = 4


def _inproj_kernel(x_ref, g_ref, w_ref, wgate_ref, fk_in, fv_in, bk_in, bv_in,
                   pbf_ref, fk_ref, fv_ref, bk_ref, bv_ref, gate_ref, *rest, with_gate_t):
    del fk_in, fv_in, bk_in, bv_in
    if with_gate_t:
        gate_t_ref, xn_ref = rest
    else:
        (xn_ref,) = rest
    j = pl.program_id(1)
    tm = x_ref.shape[0]

    @pl.when(j == 0)
    def _():
        xn = _rms(x_ref[...], g_ref[...]).astype(BF16)
        xn_ref[...] = xn
        gate = _dot(xn, wgate_ref[...])
        gate_ref[...] = gate
        if with_gate_t:
            gate_t_ref[...] = gate.T

    @pl.when(j < N_BF_BLOCKS)
    def _():
        pbf_ref[...] = _dot(xn_ref[...], w_ref[...]).astype(BF16)

    for k, ref in enumerate((fk_ref, fv_ref, bk_ref, bv_ref)):
        @pl.when(j == N_BF_BLOCKS + k)
        def _(ref=ref):
            y = _dot(xn_ref[...], w_ref[...])
            for h in range(N_HEADS):
                ref[pl.ds(h, tm, stride=N_HEADS), :] = y[:, h * HEAD_DIM:(h + 1) * HEAD_DIM]


def _inproj(x, g, w_all, w_gate, kv, *, layer, tm, with_gate_t):
    m, d = x.shape
    nb = N_BF_BLOCKS + N_F32_BLOCKS
    rows = pl.BlockSpec((None, N_HEADS * tm, HEAD_DIM), lambda i, j: (layer, i, 0))
    out_shape = [jax.ShapeDtypeStruct((m, N_BF_BLOCKS * W_MIX), BF16)]
    out_shape += [jax.ShapeDtypeStruct(a.shape, F32) for a in kv]
    out_shape += [jax.ShapeDtypeStruct((m, LANES), F32)]
    out_specs = [
        pl.BlockSpec((tm, W_MIX), lambda i, j: (i, jnp.minimum(j, N_BF_BLOCKS - 1))),
        rows, rows, rows, rows,
        pl.BlockSpec((tm, LANES), lambda i, j: (i, 0)),
    ]
    if with_gate_t:
        out_shape.append(jax.ShapeDtypeStruct((LANES, m), F32))
        out_specs.append(pl.BlockSpec((LANES, tm), lambda i, j: (0, i)))
    untouched = pl.BlockSpec(memory_space=pl.ANY)
    outs = pl.pallas_call(
        functools.partial(_inproj_kernel, with_gate_t=with_gate_t),
        grid=(m // tm, nb),
        in_specs=[
            pl.BlockSpec((tm, d), lambda i, j: (i, 0)),
            pl.BlockSpec((1, d), lambda i, j: (0, 0)),
            pl.BlockSpec((None, d, W_MIX), lambda i, j: (layer, 0, j)),
            pl.BlockSpec((None, d, LANES), lambda i, j: (layer, 0, 0)),
            untouched, untouched, untouched, untouched,
        ],
        out_specs=out_specs,
        out_shape=out_shape,
        input_output_aliases={4: 1, 5: 2, 6: 3, 7: 4},
        scratch_shapes=[pltpu.VMEM((tm, d), BF16)],
        compiler_params=_cparams("parallel", "arbitrary"),
        name="inproj",
    )(x, g, w_all, w_gate, *kv)
    return (outs[0], tuple(outs[1:5])) + tuple(outs[5:])


def _merge_kernel(x_ref, mo_ref, fo_ref, bo_ref, gm_ref, gf_ref, gb_ref, g_ref,
                  wm_ref, wf_ref, wb_ref, wo_ref, o_ref):
    merged = (_sigmoid(gm_ref[...].astype(F32)) * _dot(mo_ref[...], wm_ref[...])
              + _sigmoid(gf_ref[...].astype(F32)) * _dot(fo_ref[...], wf_ref[...])
              + _sigmoid(gb_ref[...].astype(F32)) * _dot(bo_ref[...], wb_ref[...]))
    y = _dot(merged.astype(BF16), wo_ref[...])
    o_ref[...] = x_ref[...] + _rms(y, g_ref[...])


def _merge_out(x, m_out, f_out, b_out, pbf, g, w_m, w_f, w_b, w_o, *, layer, tm):
    m, d = x.shape
    gate_blk = d // W_MIX

    def gate_spec(k):
        return pl.BlockSpec((tm, d), lambda i: (i, (6 + k * gate_blk) // gate_blk))

    row = pl.BlockSpec((tm, W_MIX), lambda i: (i, 0))
    wbr = pl.BlockSpec((None, W_MIX, d), lambda i: (layer, 0, 0))
    return pl.pallas_call(
        _merge_kernel,
        grid=(m // tm,),
        in_specs=[
            pl.BlockSpec((tm, d), lambda i: (i, 0)), row, row, row,
            gate_spec(0), gate_spec(1), gate_spec(2),
            pl.BlockSpec((1, d), lambda i: (0, 0)),
            wbr, wbr, wbr, pl.BlockSpec((None, d, d), lambda i: (layer, 0, 0)),
        ],
        out_specs=pl.BlockSpec((tm, d), lambda i: (i, 0)),
        out_shape=jax.ShapeDtypeStruct((m, d), F32),
        compiler_params=_cparams("parallel"),
        name="merge_out",
    )(x, m_out, f_out, b_out, pbf, pbf, pbf, g, w_m, w_f, w_b, w_o)


def _w_in_offsets(d_model):
    widths = (("m_q", W_MIX), ("m_k", W_MIX), ("m_v", W_MIX), ("m_o", W_MIX), ("m_i", N_HEADS), ("m_f", N_HEADS),
              ("f_q", W_MIX), ("f_k", W_MIX), ("f_v", W_MIX), ("f_f", N_HEADS),
              ("b_q", W_MIX), ("b_k", W_MIX), ("b_v", W_MIX),
              ("g_m", d_model), ("g_f", d_model), ("g_b", d_model))
    out, off = {}, 0
    for name, w in widths:
        out[name] = (off, off + w)
        off += w
    return out


def _prep_w_in(w_in):
    d = w_in.shape[-2]
    offs = _w_in_offsets(d)

    def cols(name):
        a, b = offs[name]
        return w_in[..., a:b]

    order = ("m_q", "m_k", "m_v", "m_o", "f_q", "b_q", "g_m", "g_f", "g_b", "f_k", "f_v", "b_k", "b_v")
    w_all = jnp.concatenate([cols(n) for n in order], axis=-1).astype(BF16)
    gate = jnp.concatenate([cols("m_i"), cols("m_f"), cols("f_f")], axis=-1)
    pad = [(0, 0)] * (gate.ndim - 1) + [(0, LANES - gate.shape[-1])]
    w_gate = jnp.pad(gate, pad).astype(BF16)
    return w_all, w_gate


def _gateprep_kernel(gate_ref, gate_t_ref, bias_ref, bias_t_ref,
                     act_ref, cl_ref, act_t_ref, cl_t_ref, kb_ref, qb_ref, cf_ref, *, chunk):
    t = gate_ref.shape[0]
    lane = lax.broadcasted_iota(jnp.int32, (1, LANES), 1)
    z = gate_ref[...] + bias_ref[...]
    act_ref[...] = jnp.where(lane < N_HEADS, z, _log_sigmoid(z))
    sub = lax.broadcasted_iota(jnp.int32, (LANES, 1), 0)
    zt = gate_t_ref[...] + bias_t_ref[...]
    act_t_ref[...] = jnp.where(sub < N_HEADS, zt, _log_sigmoid(zt))

    r = lax.broadcasted_iota(jnp.int32, (chunk, chunk), 0)
    c = lax.broadcasted_iota(jnp.int32, (chunk, chunk), 1)
    tril = jnp.where(c <= r, 1.0, 0.0).astype(BF16)
    triu = jnp.where(r <= c, 1.0, 0.0).astype(BF16)
    carry = jnp.zeros((1, LANES), F32)
    for ci in range(t // chunk):
        sl = slice(ci * chunk, (ci + 1) * chunk)
        loc = _dot3_r(tril, act_ref[sl, :])
        cl_ref[sl, :] = loc
        full = loc + carry
        cf_ref[sl, :] = full
        carry = full[chunk - 1:chunk, :]
        cl_t_ref[:, sl] = _dot3_l(act_t_ref[:, sl], triu)

    terms = _split3(cf_ref[...] * (HEAD_DIM ** 0.5))
    r = lax.broadcasted_iota(jnp.int32, (LANES, W_MIX), 0)
    c = lax.broadcasted_iota(jnp.int32, (LANES, W_MIX), 1)
    own = r == 2 * N_HEADS + c // HEAD_DIM
    lane = lax.broadcasted_iota(jnp.int32, (1, W_MIX), 1) % HEAD_DIM

    def place(first):
        return sum(_dot(x, jnp.where(own & (c % HEAD_DIM == first + j), 1.0, 0.0).astype(BF16))
                   for j, x in enumerate(terms))

    kb_ref[...] = jnp.where((lane >= 3) & (lane < 6), 1.0, -place(0)).astype(BF16)
    qb_ref[...] = jnp.where(lane < 3, 1.0, place(3)).astype(BF16)


def _gateprep(gate, gate_t, bias, bias_t, *, n_seq, chunk):
    t = gate.shape[0] // n_seq
    col = pl.BlockSpec((t, LANES), lambda b: (b, 0))
    row = pl.BlockSpec((LANES, t), lambda b: (0, b))
    col_s = jax.ShapeDtypeStruct(gate.shape, F32)
    row_s = jax.ShapeDtypeStruct(gate_t.shape, F32)
    dec = pl.BlockSpec((t, W_MIX), lambda b: (b, 0))
    dec_s = jax.ShapeDtypeStruct((gate.shape[0], W_MIX), BF16)
    return pl.pallas_call(
        functools.partial(_gateprep_kernel, chunk=chunk),
        grid=(n_seq,),
        in_specs=[col, row, pl.BlockSpec((1, LANES), lambda b: (0, 0)), pl.BlockSpec((LANES, 1), lambda b: (0, 0))],
        out_specs=[col, col, row, row, dec, dec],
        out_shape=[col_s, col_s, row_s, row_s, dec_s, dec_s],
        scratch_shapes=[pltpu.VMEM((t, LANES), F32)],
        compiler_params=_cparams("parallel"),
        name="gateprep",
    )(gate, gate_t, bias, bias_t)


def _mlstm_kernel(q_ref, k_ref, v_ref, o_ref, wq_ref, wk_ref, hn_ref, act_ref, cl_ref, act_t_ref, cl_t_ref,
                  out_ref, caug_ref, m_ref, cbuf, qs, kts, icb, bcb, cst, *, chunk):
    h = pl.program_id(1)
    t = q_ref.shape[0]
    reps = chunk // LANES

    def wide(a, n):
        return jnp.concatenate([a] * n, axis=1) if n > 1 else a

    def conv_silu(x_ref, w_ref):
        cbuf[0:8, :] = jnp.zeros((8, LANES), F32)
        cbuf[8:, :] = x_ref[...].astype(F32)
        y = w_ref[0:1, :] * cbuf[pl.ds(8 - (CONV_W - 1), t), :]
        for j in range(1, CONV_W):
            y = y + w_ref[j:j + 1, :] * cbuf[pl.ds(8 - (CONV_W - 1) + j, t), :]
        return y * _sigmoid(y)

    qs[...] = conv_silu(q_ref, wq_ref).astype(BF16)
    kts[...] = (conv_silu(k_ref, wk_ref) * (HEAD_DIM ** -0.5)).T.astype(BF16)

    rr = lax.broadcasted_iota(jnp.int32, (LANES, LANES), 0)
    icb[...] = _dot3_l(act_ref[...], jnp.where(rr == h, 1.0, 0.0).astype(BF16))
    bcb[...] = _dot3_l(cl_ref[...], jnp.where(rr == N_HEADS + h, 1.0, 0.0).astype(BF16))

    cst[...] = jnp.zeros_like(cst)
    tt = lax.broadcasted_iota(jnp.int32, (chunk, chunk), 0)
    ss = lax.broadcasted_iota(jnp.int32, (chunk, chunk), 1)
    ones = jnp.ones((chunk, LANES), BF16)

    def body(c, m_prev):
        r0 = pl.multiple_of(c * chunk, chunk)
        rows = pl.ds(r0, chunk)
        qc = qs[rows, :]
        ktc = kts[:, rows]
        v_aug = jnp.concatenate([v_ref[rows, :], ones], axis=1)
        i_r = act_t_ref[pl.ds(h, 1), rows]
        b_r = cl_t_ref[pl.ds(N_HEADS + h, 1), rows]
        i_c = icb[rows, :]
        b_c = bcb[rows, :]
        dmat = jnp.where(ss <= tt, wide(b_c, reps) + (i_r - b_r), -jnp.inf)
        inter = b_c + m_prev
        m_t = jnp.maximum(inter, jnp.max(dmat, axis=-1, keepdims=True))
        w_inter = jnp.exp(inter - m_t)
        s = (_dot(qc, ktc) * jnp.exp(dmat - wide(m_t, reps))).astype(BF16)
        c_prev = cst[...]
        num = wide(w_inter, 2) * _dot(qc, c_prev.astype(BF16)) + _dot(s, v_aug)
        hh = num[:, :LANES] / jnp.maximum(jnp.abs(num[:, LANES:]), jnp.exp(-m_t))
        hh = hh * lax.rsqrt(jnp.mean(hh * hh, axis=-1, keepdims=True) + EPS) * hn_ref[...]
        out_ref[rows, :] = (_sigmoid(o_ref[rows, :].astype(F32)) * hh).astype(BF16)

        b_last = b_r[:, chunk - 1:chunk]
        m_new = jnp.maximum(b_last + m_prev, jnp.max(b_last - b_r + i_r, axis=-1, keepdims=True))
        a_prev = jnp.exp(b_last + m_prev - m_new)
        wg_c = jnp.exp(b_last - b_c + i_c - m_new)
        wv = (wide(wg_c, 2) * v_aug.astype(F32)).astype(BF16)
        cst[...] = a_prev * c_prev + _dot(ktc, wv)
        return m_new

    m_fin = lax.fori_loop(0, t // chunk, body, jnp.full((1, 1), -jnp.inf, F32))
    caug_ref[...] = cst[...]
    m_ref[...] = jnp.broadcast_to(m_fin, m_ref.shape)


def _mlstm_prompt(pbf, w_conv, g_hn, act, cl, act_t, cl_t, *, n_seq, chunk):
    bt = pbf.shape[0]
    t = bt // n_seq

    def head_cols(blk):
        return pl.BlockSpec((t, HEAD_DIM), lambda b, h: (b, blk * N_HEADS + h))

    col = pl.BlockSpec((t, LANES), lambda b, h: (b, 0))
    row = pl.BlockSpec((LANES, t), lambda b, h: (0, b))
    return pl.pallas_call(
        functools.partial(_mlstm_kernel, chunk=chunk),
        grid=(n_seq, N_HEADS),
        in_specs=[
            head_cols(0), head_cols(1), head_cols(2), head_cols(3),
            pl.BlockSpec((CONV_W, HEAD_DIM), lambda b, h: (0, h)),
            pl.BlockSpec((CONV_W, HEAD_DIM), lambda b, h: (0, N_HEADS + h)),
            pl.BlockSpec((1, HEAD_DIM), lambda b, h: (0, h)),
            col, col, row, row,
        ],
        out_specs=[
            pl.BlockSpec((t, HEAD_DIM), lambda b, h: (b, h)),
            pl.BlockSpec((None, None, HEAD_DIM, 2 * LANES), lambda b, h: (b, h, 0, 0)),
            pl.BlockSpec((None, None, 8, LANES), lambda b, h: (b, h, 0, 0)),
        ],
        out_shape=[
            jax.ShapeDtypeStruct((bt, W_MIX), BF16),
            jax.ShapeDtypeStruct((n_seq, N_HEADS, HEAD_DIM, 2 * LANES), F32),
            jax.ShapeDtypeStruct((n_seq, N_HEADS, 8, LANES), F32),
        ],
        scratch_shapes=[
            pltpu.VMEM((t + 8, LANES), F32), pltpu.VMEM((t, HEAD_DIM), BF16), pltpu.VMEM((HEAD_DIM, t), BF16),
            pltpu.VMEM((t, LANES), F32), pltpu.VMEM((t, LANES), F32), pltpu.VMEM((HEAD_DIM, 2 * LANES), F32),
        ],
        compiler_params=_cparams("parallel", "arbitrary"),
        name="mlstm_prompt",
    )(pbf, pbf, pbf, pbf, w_conv, w_conv, g_hn, act, cl, act_t, cl_t)


def _attn_kernel(*refs, moba):
    if moba:
        q_ref, k_ref, v_ref, o_ref, kaug, vt, qa0, pb0, qa1, pb1, kmean, sel0, sel1 = refs
        slots = ((qa0, pb0, sel0), (qa1, pb1, sel1))
    else:
        q_ref, k_ref, v_ref, kb_ref, qb_ref, o_ref, kaug, vt, qa0, pb0, qa1, pb1 = refs
        slots = ((qa0, pb0, None), (qa1, pb1, None))
    jp = pl.program_id(2)
    t = kaug.shape[0]
    blk = qa0.shape[0]
    n_blk = t // blk
    scale = HEAD_DIM ** -0.5
    c_exp = scale * math.log2(math.e)

    @pl.when(jp == 0)
    def _():
        own_rows = pl.ds(pl.program_id(1), t, stride=N_HEADS)
        k = k_ref[own_rows, :]
        kaug[:, :HEAD_DIM] = k.astype(BF16)
        vt[...] = v_ref[own_rows, :].T.astype(BF16)
        if moba:
            lane = lax.broadcasted_iota(jnp.int32, (t, LANES), 1)
            kblk = lax.broadcasted_iota(jnp.int32, (t, LANES), 0) // blk
            kaug[:, HEAD_DIM:] = jnp.where(lane == kblk, 1.0, 0.0).astype(BF16)
            kmean[...] = jnp.zeros_like(kmean)
            for n in range(n_blk):
                kmean[n:n + 1, :] = jnp.mean(k[n * blk:(n + 1) * blk, :], axis=0, keepdims=True)
        else:
            kaug[:, HEAD_DIM:] = kb_ref[...]

    ks = lax.broadcasted_iota(jnp.int32, (blk, blk), 0)
    qs = lax.broadcasted_iota(jnp.int32, (blk, blk), 1)

    def attend(qi, slot):
        qaug, pbuf, selb = slots[slot]
        rows = slice(qi * blk, (qi + 1) * blk)
        qb = q_ref[rows, :]
        qaug[:, :HEAD_DIM] = qb
        if moba:
            nbp = kmean.shape[0]
            row = lax.broadcasted_iota(jnp.int32, (nbp, blk), 0)
            km_hi, km_mid, km_lo = _split3(kmean[...])
            gs = _dot_nt(km_hi, qb) + _dot_nt(km_mid, qb) + _dot_nt(km_lo, qb)
            gm = jnp.where(row < qi, gs, -jnp.inf)
            selb[...] = jnp.zeros_like(selb)
            for n in range(qi):
                g_n = gm[n:n + 1, :]
                beats = (gm > g_n) | ((gm == g_n) & (row < n))
                cnt = jnp.sum(jnp.where(beats, 1.0, 0.0), axis=0, keepdims=True)
                selb[n:n + 1, :] = jnp.where(cnt < MOBA_TOPK, 0.0, NEG / scale)
            qaug[:, HEAD_DIM:] = selb[...].T.astype(BF16)
        else:
            qaug[:, HEAD_DIM:] = qb_ref[rows, :]
        qa = qaug[...]

        def raw(c):
            s = _dot_nt(kaug[c * blk:(c + 1) * blk, :], qa)
            return jnp.where(ks <= qs, s, NEG) if c == qi else s

        m = jnp.max(raw(qi), axis=0, keepdims=True)
        for c in range(qi):
            m = jnp.maximum(m, jnp.max(raw(c), axis=0, keepdims=True))
        l = jnp.zeros((1, blk), F32)
        for c in range(qi + 1):
            p = jnp.exp2((raw(c) - m) * c_exp)
            l = l + jnp.sum(p, axis=0, keepdims=True)
            pbuf[c * blk:(c + 1) * blk, :] = p.astype(BF16)
        nk = (qi + 1) * blk
        acc = _dot(vt[:, 0:nk], pbuf[0:nk, :])
        o_ref[rows, :] = (acc / l).T.astype(BF16)

    def pair(j):
        attend(j, 0)
        if n_blk - 1 - j != j:
            attend(n_blk - 1 - j, 1)

    for j in range(-(-n_blk // 2)):
        pl.when(jp == j)(functools.partial(pair, j))


def _attn_prompt(pbf, k, v, kbias, qbias, *, layer, n_seq, q_block, moba):
    bt = pbf.shape[0]
    t = bt // n_seq
    blk = MOBA_BLOCK
    nq = t // blk
    assert nq <= LANES
    head = pl.BlockSpec((t, HEAD_DIM), lambda b, h, j: (b, h))
    kv = pl.BlockSpec((None, N_HEADS * t, HEAD_DIM), lambda b, h, j: (layer, b, 0))
    in_specs = [pl.BlockSpec((t, HEAD_DIM), lambda b, h, j: (b, q_block * N_HEADS + h)), kv, kv]
    args = [pbf, k, v]
    slot = [pltpu.VMEM((blk, 2 * HEAD_DIM), BF16), pltpu.VMEM((t, blk), BF16)]
    scratch = [pltpu.VMEM((t, 2 * HEAD_DIM), BF16), pltpu.VMEM((HEAD_DIM, t), BF16)] + slot + slot
    if moba:
        nbp = -(-nq // 8) * 8
        scratch += [pltpu.VMEM((nbp, HEAD_DIM), F32), pltpu.VMEM((LANES, blk), F32), pltpu.VMEM((LANES, blk), F32)]
    else:
        in_specs += [head, head]
        args += [kbias, qbias]
    return pl.pallas_call(
        functools.partial(_attn_kernel, moba=moba),
        grid=(n_seq, N_HEADS, -(-nq // 2)),
        in_specs=in_specs,
        out_specs=head,
        out_shape=jax.ShapeDtypeStruct((bt, W_MIX), BF16),
        scratch_shapes=scratch,
        compiler_params=_cparams("parallel", "arbitrary", "arbitrary"),
        name="moba_prompt" if moba else "fox_prompt",
    )(*args)


def _mlstm_step_kernel(p_ref, gate_ref, bias_ref, conv_ref, wc_ref, hn_ref, c0_ref, n0_ref, m0_ref,
                       out_ref, c_ref, n_ref, m_ref):
    x = p_ref[...].astype(F32)
    qk = wc_ref[CONV_W - 1:CONV_W, :] * x[:, :2 * W_MIX]
    for j in range(CONV_W - 1):
        qk = qk + wc_ref[j:j + 1, :] * conv_ref[j:j + 1, :]
    qk = qk * _sigmoid(qk)
    q = qk[:, :W_MIX]
    k = qk[:, W_MIX:] * (HEAD_DIM ** -0.5)
    v = x[:, 2 * W_MIX:3 * W_MIX]
    og = _sigmoid(x[:, 3 * W_MIX:])
    z = gate_ref[...] + bias_ref[...]
    logf = _log_sigmoid(z)
    lane = lax.broadcasted_iota(jnp.int32, (1, LANES), 1)
    m_row = jnp.zeros((1, LANES), F32)
    outs = []
    for h in range(N_HEADS):
        hs = slice(h * HEAD_DIM, (h + 1) * HEAD_DIM)
        i_h = z[:, h:h + 1]
        f_h = logf[:, N_HEADS + h:N_HEADS + h + 1]
        m0 = m0_ref[:, h:h + 1]
        m_new = jnp.maximum(f_h + m0, i_h)
        a = jnp.exp(f_h + m0 - m_new)
        wgt = jnp.exp(i_h - m_new)
        k_col = jnp.broadcast_to(k[:, hs], (HEAD_DIM, HEAD_DIM)).T
        q_col = jnp.broadcast_to(q[:, hs], (HEAD_DIM, HEAD_DIM)).T
        c_new = a * c0_ref[h] + (wgt * k_col) * v[:, hs]
        n_new = a * n0_ref[h:h + 1, :] + wgt * k[:, hs]
        c_ref[h] = c_new
        n_ref[h:h + 1, :] = n_new
        num = jnp.sum(q_col * c_new, axis=0, keepdims=True)
        den = jnp.sum(q[:, hs] * n_new, axis=-1, keepdims=True)
        hh = num / jnp.maximum(jnp.abs(den), jnp.exp(-m_new))
        hh = hh * lax.rsqrt(jnp.mean(hh * hh, axis=-1, keepdims=True) + EPS) * hn_ref[:, hs]
        outs.append(og[:, hs] * hh)
        m_row = jnp.where(lane == h, m_new, m_row)
    out_ref[...] = jnp.concatenate(outs, axis=1).astype(BF16)
    m_ref[...] = m_row


def _mlstm_step(pbf3, gate3, bias, conv_prev, w_conv, g_hn, c0, n0, m0):
    bs = pbf3.shape[0]
    return pl.pallas_call(
        _mlstm_step_kernel,
        grid=(bs,),
        in_specs=[
            pl.BlockSpec((None, 1, 4 * W_MIX), lambda b: (b, 0, 0)),
            pl.BlockSpec((None, 1, LANES), lambda b: (b, 0, 0)),
            pl.BlockSpec((1, LANES), lambda b: (0, 0)),
            pl.BlockSpec((None, CONV_W - 1, 2 * W_MIX), lambda b: (b, 0, 0)),
            pl.BlockSpec((CONV_W, 2 * W_MIX), lambda b: (0, 0)),
            pl.BlockSpec((1, W_MIX), lambda b: (0, 0)),
            pl.BlockSpec((None, N_HEADS, HEAD_DIM, HEAD_DIM), lambda b: (b, 0, 0, 0)),
            pl.BlockSpec((None, N_HEADS, HEAD_DIM), lambda b: (b, 0, 0)),
            pl.BlockSpec((None, 1, N_HEADS), lambda b: (b, 0, 0)),
        ],
        out_specs=[
            pl.BlockSpec((None, 1, W_MIX), lambda b: (b, 0, 0)),
            pl.BlockSpec((None, N_HEADS, HEAD_DIM, HEAD_DIM), lambda b: (b, 0, 0, 0)),
            pl.BlockSpec((None, N_HEADS, HEAD_DIM), lambda b: (b, 0, 0)),
            pl.BlockSpec((None, 1, LANES), lambda b: (b, 0, 0)),
        ],
        out_shape=[
            jax.ShapeDtypeStruct((bs, 1, W_MIX), BF16),
            jax.ShapeDtypeStruct(c0.shape, F32),
            jax.ShapeDtypeStruct(n0.shape, F32),
            jax.ShapeDtypeStruct((bs, 1, LANES), F32),
        ],
        compiler_params=_cparams("parallel"),
        name="mlstm_step",
    )(pbf3, gate3, bias, conv_prev, w_conv, g_hn, c0, n0, m0)


PAGE_ROWS = PAGE * N_HEADS


def _pad_rows8(a):
    return jnp.concatenate([a, jnp.zeros_like(a)], axis=0)


def _decode_kernel(pt_ref, *refs, group, n_pages):
    del pt_ref
    pages = refs[:4 * group]
    (fq_ref, bq_ref, knew_ref, vnew_ref, gate_ref, bias_ref,
     out_ref, lf_ref, top_ref, qf, qb, m_s, l_s, acc, carry, ksum, msuf, mtot) = refs[4 * group:]
    first = (pl.program_id(0) == 0) & (pl.program_id(1) == 0)
    g = pl.program_id(1)
    n_blocks = n_pages // 2
    scale = HEAD_DIM ** -0.5

    @pl.when(first)
    def _():
        src = lax.broadcasted_iota(jnp.int32, (PAGE_ROWS, PAGE_ROWS), 0)
        dst = lax.broadcasted_iota(jnp.int32, (PAGE_ROWS, PAGE_ROWS), 1)
        same_head = (src & (N_HEADS - 1)) == (dst & (N_HEADS - 1))
        mtot[...] = jnp.where(same_head, 1.0, 0.0).astype(BF16)
        msuf[...] = jnp.where(same_head & (src > dst), 1.0, 0.0).astype(BF16)

    @pl.when(g == 0)
    def _():
        qf[...] = _pad_rows8(fq_ref[...].astype(F32))
        qb[...] = _pad_rows8(bq_ref[...].astype(F32))
        m_s[...] = jnp.full(m_s.shape, NEG, F32)
        l_s[...] = jnp.zeros_like(l_s)
        acc[...] = jnp.zeros_like(acc)
        lf_new = _log_sigmoid(gate_ref[...] + bias_ref[...])
        lf_ref[...] = lf_new
        r = lax.broadcasted_iota(jnp.int32, (LANES, PAGE_ROWS), 0)
        c = lax.broadcasted_iota(jnp.int32, (LANES, PAGE_ROWS), 1)
        spread = jnp.where(r == 2 * N_HEADS + (c & (N_HEADS - 1)), 1.0, 0.0).astype(BF16)
        carry[...] = _dot3_l(jnp.broadcast_to(lf_new, (8, LANES)), spread)[0:1]
        ksum[...] = jnp.zeros_like(ksum)

    row = lax.broadcasted_iota(jnp.int32, (8, PAGE_ROWS), 0)
    lane = lax.broadcasted_iota(jnp.int32, (8, PAGE_ROWS), 1)
    own_head = (lane & (N_HEADS - 1)) == row
    wide = lambda a: jnp.concatenate([a] * N_HEADS, axis=1)

    lf_rows = jnp.concatenate([pages[4 * i + 3][...] for i in range(group)], axis=0)
    suf = _dot3_l(lf_rows, msuf[...])
    tot = _dot3_l(lf_rows, mtot[...])
    run = carry[...]
    q_f = qf[...].astype(BF16)
    s_all = []
    for i in range(group):
        s = _dot_nt(q_f, pages[4 * i][...].astype(BF16)) * scale + (suf[i:i + 1] + run)
        s_all.append(jnp.where(own_head, s, NEG))
        run = run + tot[i:i + 1]
    carry[...] = run
    m_old = m_s[...]
    m_new = m_old
    for s in s_all:
        m_new = jnp.maximum(m_new, jnp.max(s, axis=-1, keepdims=True))
    alpha = jnp.exp(m_old - m_new)
    l_new = alpha * l_s[...]
    a_new = alpha * acc[...]
    m_wide = wide(m_new)
    for i, s in enumerate(s_all):
        p = jnp.exp(s - m_wide)
        l_new = l_new + jnp.sum(p, axis=-1, keepdims=True)
        a_new = a_new + _dot(p.astype(BF16), pages[4 * i + 1][...].astype(BF16))
    m_s[...] = m_new
    l_s[...] = l_new
    acc[...] = a_new

    sub8 = lax.broadcasted_iota(jnp.int32, (8, HEAD_DIM), 0)
    for i in range(group):
        vs = jnp.sum(pages[4 * i + 2][...].reshape(PAGE_ROWS // 8, 8, HEAD_DIM), axis=0)
        per_head = jnp.where(sub8 < N_HEADS, vs + pltpu.roll(vs, N_HEADS, 0), 0.0)
        r0 = pl.multiple_of((n_blocks - 1 - (g * group + i) // 2) * 8, 8)
        ksum[pl.ds(r0, 8), :] = ksum[pl.ds(r0, 8), :] + per_head

    @pl.when(g == pl.num_programs(1) - 1)
    def _():
        s_new = jnp.sum(qf[...] * _pad_rows8(knew_ref[...]), axis=-1, keepdims=True) * scale
        m_old = m_s[...]
        m_fin = jnp.maximum(m_old, s_new)
        alpha = jnp.exp(m_old - m_fin)
        p_new = jnp.exp(s_new - m_fin)
        l_fin = alpha * l_s[...] + p_new
        o = (alpha * acc[...] + p_new * _pad_rows8(vnew_ref[...])) / l_fin
        out_ref[...] = o[0:N_HEADS].astype(BF16)
        km_hi, km_mid, km_lo = _split3(ksum[...] * (1.0 / MOBA_BLOCK))
        q_b = qb[...].astype(BF16)
        gs = _dot_nt(q_b, km_hi) + _dot_nt(q_b, km_mid) + _dot_nt(q_b, km_lo)
        grow = lax.broadcasted_iota(jnp.int32, gs.shape, 0)
        glane = lax.broadcasted_iota(jnp.int32, gs.shape, 1)
        gs = jnp.where((glane & 7) == grow, gs, -jnp.inf)
        glane_f = glane.astype(F32)
        tlane = lax.broadcasted_iota(jnp.int32, (8, LANES), 1)
        top = jnp.zeros((8, LANES), F32)
        for slot in range(MOBA_TOPK):
            best = jnp.max(gs, axis=-1, keepdims=True)
            idx = jnp.min(jnp.where(gs == best, glane_f, float(8 * n_blocks)), axis=-1, keepdims=True)
            top = jnp.where(tlane == slot, idx, top)
            gs = jnp.where(glane_f == idx, -jnp.inf, gs)
        top_ref[...] = jnp.right_shift(top.astype(jnp.int32), 3)


def _decode_attn(page_table, cfk, cfv, cbk, lfc, fq4, bq4, fk4, fv4, gate3, bias, *, layer, group):
    bs, n_pages = page_table.shape
    assert n_pages % group == 0 and group % 2 == 0 and n_pages // 2 >= MOBA_TOPK

    def page_spec(i, shape):
        def imap(b, g, pt):
            return (layer, pt[b * n_pages + (n_pages - 1 - (g * group + i))], 0, 0)
        return pl.BlockSpec((None, None) + shape, imap)

    in_specs, args = [], []
    for i in range(group):
        in_specs += [page_spec(i, (PAGE_ROWS, HEAD_DIM)), page_spec(i, (PAGE_ROWS, HEAD_DIM)),
                     page_spec(i, (PAGE_ROWS, HEAD_DIM)), page_spec(i, (1, PAGE_ROWS))]
        args += [cfk, cfv, cbk, lfc]
    per_head = pl.BlockSpec((None, N_HEADS, HEAD_DIM), lambda b, g, pt: (b, 0, 0))
    in_specs += [per_head, per_head, per_head, per_head,
                 pl.BlockSpec((None, 1, LANES), lambda b, g, pt: (b, 0, 0)),
                 pl.BlockSpec((1, LANES), lambda b, g, pt: (0, 0))]
    args += [fq4, bq4, fk4, fv4, gate3, bias]
    small = pltpu.VMEM((8, LANES), F32)
    return pl.pallas_call(
        functools.partial(_decode_kernel, group=group, n_pages=n_pages),
        grid_spec=pltpu.PrefetchScalarGridSpec(
            num_scalar_prefetch=1,
            grid=(bs, n_pages // group),
            in_specs=in_specs,
            out_specs=[
                per_head,
                pl.BlockSpec((None, 1, LANES), lambda b, g, pt: (b, 0, 0)),
                pl.BlockSpec((None, 8, LANES), lambda b, g, pt: (b, 0, 0)),
            ],
            scratch_shapes=[
                small, small, small, small, small,
                pltpu.VMEM((1, PAGE_ROWS), F32),
                pltpu.VMEM((8 * (n_pages // 2), HEAD_DIM), F32),
                pltpu.VMEM((PAGE_ROWS, PAGE_ROWS), BF16), pltpu.VMEM((PAGE_ROWS, PAGE_ROWS), BF16),
            ],
        ),
        out_shape=[
            jax.ShapeDtypeStruct((bs, N_HEADS, HEAD_DIM), BF16),
            jax.ShapeDtypeStruct((bs, 1, LANES), F32),
            jax.ShapeDtypeStruct((bs, 8, LANES), jnp.int32),
        ],
        compiler_params=_cparams("arbitrary", "arbitrary"),
        name="decode_attn",
    )(page_table.reshape(-1), *args)


MOBA_PAGES = 2 * MOBA_TOPK


def _moba_decode_kernel(pg_ref, k_hbm, v_hbm, q_ref, knew_ref, vnew_ref, out_ref, kbuf, vbuf, sem, *, layer):
    b = pl.program_id(0)
    slot = b % 2
    scale = HEAD_DIM ** -0.5

    def copies(seq, to_slot):
        out = []
        for h in range(N_HEADS):
            for j in range(MOBA_PAGES):
                i = h * MOBA_PAGES + j
                page = pg_ref[seq * N_HEADS * MOBA_PAGES + i]
                out.append(pltpu.make_async_copy(k_hbm.at[layer, page, :, h, :], kbuf.at[to_slot, i], sem.at[to_slot]))
                out.append(pltpu.make_async_copy(v_hbm.at[layer, page, :, h, :], vbuf.at[to_slot, i], sem.at[to_slot]))
        return out

    @pl.when(b == 0)
    def _():
        for c in copies(0, 0):
            c.start()

    @pl.when(b + 1 < pl.num_programs(0))
    def _():
        for c in copies(b + 1, 1 - slot):
            c.start()

    for c in copies(b, slot):
        c.wait()

    q = q_ref[...]
    rows = []
    for h in range(N_HEADS):
        q_h = q[h:h + 1, :]
        q8 = jnp.broadcast_to(q_h, (8, HEAD_DIM)).astype(BF16)
        pages = [h * MOBA_PAGES + i for i in range(MOBA_PAGES)]
        s = [_dot_nt(q8, kbuf[slot, i].astype(BF16))[0:1] * scale for i in pages]
        s_new = jnp.sum(q_h * knew_ref[h:h + 1, :], axis=-1, keepdims=True) * scale
        m = s_new
        for si in s:
            m = jnp.maximum(m, jnp.max(si, axis=-1, keepdims=True))
        p_new = jnp.exp(s_new - m)
        l = p_new
        o = p_new * vnew_ref[h:h + 1, :]
        for si, i in zip(s, pages):
            p = jnp.exp(si - m)
            l = l + jnp.sum(p, axis=-1, keepdims=True)
            o = o + _dot(jnp.broadcast_to(p, (8, PAGE)).astype(BF16), vbuf[slot, i].astype(BF16))[0:1]
        rows.append(o / l)
    out_ref[...] = jnp.concatenate(rows, axis=0).astype(BF16)


def _moba_decode(pages, cache_k, cache_v, bq, bk, bv, *, layer):
    bs = pages.shape[0]
    per_seq = pl.BlockSpec((None, N_HEADS, HEAD_DIM), lambda b, pg: (b, 0, 0))
    hbm = pl.BlockSpec(memory_space=pl.ANY)
    buf = pltpu.VMEM((2, N_HEADS * MOBA_PAGES, PAGE, HEAD_DIM), F32)
    return pl.pallas_call(
        functools.partial(_moba_decode_kernel, layer=layer),
        grid_spec=pltpu.PrefetchScalarGridSpec(
            num_scalar_prefetch=1,
            grid=(bs,),
            in_specs=[hbm, hbm, per_seq, per_seq, per_seq],
            out_specs=per_seq,
            scratch_shapes=[buf, buf, pltpu.SemaphoreType.DMA((2,))],
        ),
        out_shape=jax.ShapeDtypeStruct((bs, N_HEADS, HEAD_DIM), BF16),
        compiler_params=_cparams("arbitrary"),
        name="moba_decode",
    )(pages.reshape(-1), cache_k, cache_v, bq, bk, bv)


ROW_TILE = 1024
MERGE_TILE = 512
FF_TILE = 256
MLSTM_CHUNK = 256
DECODE_GROUP = 16


def kernel(x_prompt, x_sample, cache_fox_k, cache_fox_v, cache_fox_logf, cache_moba_k, cache_moba_v,
           state_mlstm_C, state_mlstm_n, state_mlstm_m, state_mlstm_conv, page_table,
           norm_g, w_ffn1_gu, w_ffn1_d, w_ffn2_gu, w_ffn2_d, w_in, b_mlstm_i, b_mlstm_f, b_fox_f,
           w_conv, g_headnorm, w_br_m, w_br_f, w_br_b, w_out):
    n_seq, t, d = x_prompt.shape
    bs, dec_t, _ = x_sample.shape
    depth, n_pool = cache_fox_k.shape[:2]
    assert dec_t == 1 and t % MOBA_BLOCK == 0 and cache_fox_k.shape[2:] == (PAGE, N_HEADS, HEAD_DIM)
    assert page_table.shape[1] % DECODE_GROUP == 0
    chunk = math.gcd(t, MLSTM_CHUNK)
    tm = math.gcd(n_seq * t, ROW_TILE)
    tm_merge = math.gcd(n_seq * t, MERGE_TILE)
    tf = math.gcd(w_ffn1_d.shape[1], FF_TILE)

    w_all, w_gate = _prep_w_in(w_in)
    wgu1, wd1, wgu2, wd2 = (w.astype(BF16) for w in (w_ffn1_gu, w_ffn1_d, w_ffn2_gu, w_ffn2_d))
    wbm, wbf, wbb, wo = (w.astype(BF16) for w in (w_br_m, w_br_f, w_br_b, w_out))
    bias = jnp.pad(jnp.concatenate([b_mlstm_i, b_mlstm_f, b_fox_f], axis=-1),
                   ((0, 0), (0, LANES - 3 * N_HEADS)))[:, None, :]
    bias_t = jnp.swapaxes(bias, 1, 2)
    paged = lambda c: c.reshape(depth, n_pool, PAGE_ROWS, HEAD_DIM)
    cfk, cfv, cbk = paged(cache_fox_k), paged(cache_fox_v), paged(cache_moba_k)
    lfc = cache_fox_logf.reshape(depth, n_pool, 1, PAGE_ROWS)
    hn = g_headnorm[:, None, :]

    xp = x_prompt.reshape(n_seq * t, d)
    xs = x_sample.reshape(bs, d)
    new_p, new_s = [], []
    kv_p = tuple(jnp.zeros((depth, N_HEADS * n_seq * t, HEAD_DIM), F32) for _ in range(4))
    kv_s = tuple(jnp.zeros((depth, N_HEADS * bs, HEAD_DIM), F32) for _ in range(4))
    for l in range(depth):
        g = [norm_g[l, i][None, :] for i in range(6)]
        xp = _ffn_block(xp, g[0], g[1], wgu1, wd1, layer=l, tm=tm, tf=tf)
        pbf, kv_p, gate, gate_t = _inproj(xp, g[2], w_all, w_gate, kv_p, layer=l, tm=tm, with_gate_t=True)
        act, cl, act_t, cl_t, kbias, qbias = _gateprep(gate, gate_t, bias[l], bias_t[l], n_seq=n_seq, chunk=chunk)
        m_out, caug, m_fin = _mlstm_prompt(pbf, w_conv[l], hn[l], act, cl, act_t, cl_t, n_seq=n_seq, chunk=chunk)
        f_out = _attn_prompt(pbf, kv_p[0], kv_p[1], kbias, qbias, layer=l, n_seq=n_seq, q_block=4, moba=False)
        b_out = _attn_prompt(pbf, kv_p[2], kv_p[3], None, None, layer=l, n_seq=n_seq, q_block=5, moba=True)
        xp = _merge_out(xp, m_out, f_out, b_out, pbf, g[3], wbm, wbf, wbb, wo, layer=l, tm=tm_merge)
        xp = _ffn_block(xp, g[4], g[5], wgu2, wd2, layer=l, tm=tm, tf=tf)
        conv_p = pbf.reshape(n_seq, t, -1)[:, t - (CONV_W - 1):, :2 * W_MIX].astype(F32)
        new_p.append((act[:, 2 * N_HEADS:3 * N_HEADS].reshape(n_seq, t, N_HEADS),
                      caug[..., :HEAD_DIM], caug[..., HEAD_DIM], m_fin[:, :, 0, 0], conv_p))
        xs = _ffn_block(xs, g[0], g[1], wgu1, wd1, layer=l, tm=bs, tf=tf)
        pbf_s, kv_s, gate_s = _inproj(xs, g[2], w_all, w_gate, kv_s, layer=l, tm=bs, with_gate_t=False)
        row3 = lambda a: a.reshape(bs, 1, -1)
        pbf3, gate3 = row3(pbf_s), row3(gate_s)
        m_out_s, c_s, n_s, m_s = _mlstm_step(pbf3, gate3, bias[l], state_mlstm_conv[l], w_conv[l], hn[l],
                                             state_mlstm_C[l], state_mlstm_n[l], state_mlstm_m[l][:, None, :])
        h3 = lambda a: a.reshape(bs, N_HEADS, HEAD_DIM)
        fq_s, bq_s = pbf_s[:, 4 * W_MIX:5 * W_MIX], pbf_s[:, 5 * W_MIX:6 * W_MIX]
        f_out_s, lf_s, top = _decode_attn(page_table, cfk, cfv, cbk, lfc, h3(fq_s), h3(bq_s), h3(kv_s[0][l]), h3(kv_s[1][l]),
                                          gate3, bias[l], layer=l, group=DECODE_GROUP)
        blk_pages = (2 * top[:, :N_HEADS, :MOBA_TOPK, None] + jnp.arange(2)).reshape(bs, N_HEADS * MOBA_PAGES)
        b_out_s = _moba_decode(jnp.take_along_axis(page_table, blk_pages, axis=1), cache_moba_k, cache_moba_v,
                               h3(bq_s).astype(F32), h3(kv_s[2][l]), h3(kv_s[3][l]), layer=l)
        flat = lambda a: a.reshape(bs, -1)
        xs = _merge_out(xs, flat(m_out_s), flat(f_out_s), flat(b_out_s), pbf_s, g[3], wbm, wbf, wbb, wo, layer=l, tm=bs)
        xs = _ffn_block(xs, g[4], g[5], wgu2, wd2, layer=l, tm=bs, tf=tf)
        conv_s = jnp.concatenate([state_mlstm_conv[l][:, 1:], pbf_s[:, None, :2 * W_MIX].astype(F32)], axis=1)
        new_s.append((lf_s[:, :, 2 * N_HEADS:3 * N_HEADS], c_s, n_s, m_s[:, 0, :N_HEADS], conv_s))
    logf_p, c_p, n_p, m_p, conv_p = [jnp.stack(z) for z in zip(*new_p)]
    logf_s, c_s, n_s, m_s, conv_s = [jnp.stack(z) for z in zip(*new_s)]
    kv_p = [a.reshape(depth, n_seq, t, N_HEADS, HEAD_DIM) for a in kv_p]
    kv_s = [a.reshape(depth, bs, 1, N_HEADS, HEAD_DIM) for a in kv_s]
    return (xp.reshape(n_seq, t, d), xs.reshape(bs, 1, d),
            kv_p[0], kv_p[1], logf_p, kv_p[2], kv_p[3], c_p, n_p, m_p, conv_p,
            kv_s[0], kv_s[1], logf_s, kv_s[2], kv_s[3], c_s, n_s, m_s, conv_s)
```

```python
import functools
import math

import jax
import jax.numpy as jnp
from jax import lax
from jax.experimental import pallas as pl
from jax.experimental.pallas import tpu as pltpu

F32 = jnp.float32
BF16 = jnp.bfloat16
EPS = 1e-6
NEG = -1e30

HEAD_DIM = 128
N_HEADS = 4
W_MIX = N_HEADS * HEAD_DIM
CONV_W = 4
MOBA_BLOCK = 256
MOBA_TOPK = 3
PAGE = 128
LANES = 128
VMEM_LIMIT = 52 * 1024 * 1024


def _cparams(*sem):
    return pltpu.CompilerParams(dimension_semantics=sem, vmem_limit_bytes=VMEM_LIMIT)


def _rms(x, g):
    return x * lax.rsqrt(jnp.mean(x * x, axis=-1, keepdims=True) + EPS) * g


def _split3(a):
    hi = a.astype(BF16)
    r = a - hi.astype(F32)
    mid = r.astype(BF16)
    lo = (r - mid.astype(F32)).astype(BF16)
    return hi, mid, lo


def _dot(a, b):
    return jnp.dot(a, b, preferred_element_type=F32)


def _dot_nt(a, b):
    return lax.dot_general(a, b, (((1,), (1,)), ((), ())), preferred_element_type=F32)


def _dot3_l(a_f32, b_bf16):
    hi, mid, lo = _split3(a_f32)
    return _dot(hi, b_bf16) + _dot(mid, b_bf16) + _dot(lo, b_bf16)


def _dot3_r(a_bf16, b_f32):
    hi, mid, lo = _split3(b_f32)
    return _dot(a_bf16, hi) + _dot(a_bf16, mid) + _dot(a_bf16, lo)


def _log_sigmoid(z):
    return jnp.minimum(z, 0.0) - jnp.log1p(jnp.exp(-jnp.abs(z)))


def _sigmoid(z):
    return 1.0 / (1.0 + jnp.exp(-z))


def _ffn_kernel(x_ref, ga_ref, gb_ref, wg_ref, wu_ref, wd_ref, o_ref, xn_ref, acc_ref):
    j = pl.program_id(1)

    @pl.when(j == 0)
    def _():
        xn_ref[...] = _rms(x_ref[...], ga_ref[...]).astype(BF16)
        acc_ref[...] = jnp.zeros_like(acc_ref)

    xn = xn_ref[...]
    g = _dot(xn, wg_ref[...])
    u = _dot(xn, wu_ref[...])
    h = (g * _sigmoid(g) * u).astype(BF16)
    acc_ref[...] += _dot(h, wd_ref[...])

    @pl.when(j == pl.num_programs(1) - 1)
    def _():
        o_ref[...] = x_ref[...] + 0.5 * _rms(acc_ref[...], gb_ref[...])


def _ffn_block(x, ga, gb, w_gu, w_d, *, layer, tm, tf):
    m, d = x.shape
    f = w_d.shape[1]
    nj = f // tf
    return pl.pallas_call(
        _ffn_kernel,
        grid=(m // tm, nj),
        in_specs=[
            pl.BlockSpec((tm, d), lambda i, j: (i, 0)),
            pl.BlockSpec((1, d), lambda i, j: (0, 0)),
            pl.BlockSpec((1, d), lambda i, j: (0, 0)),
            pl.BlockSpec((None, d, tf), lambda i, j: (layer, 0, j)),
            pl.BlockSpec((None, d, tf), lambda i, j: (layer, 0, j + nj)),
            pl.BlockSpec((None, tf, d), lambda i, j: (layer, j, 0)),
        ],
        out_specs=pl.BlockSpec((tm, d), lambda i, j: (i, 0)),
        out_shape=jax.ShapeDtypeStruct((m, d), F32),
        scratch_shapes=[pltpu.VMEM((tm, d), BF16), pltpu.VMEM((tm, d), F32)],
        compiler_params=_cparams("parallel", "arbitrary"),
        name="ffn_block",
    )(x, ga, gb, w_gu, w_gu, w_d)


N_BF_BLOCKS = 12
N_F32_BLOCKS = 4


def _inproj_kernel(x_ref, g_ref, w_ref, wgate_ref, fk_in, fv_in, bk_in, bv_in,
                   pbf_ref, fk_ref, fv_ref, bk_ref, bv_ref, gate_ref, *rest, with_gate_t):
    del fk_in, fv_in, bk_in, bv_in
    if with_gate_t:
        gate_t_ref, xn_ref = rest
    else:
        (xn_ref,) = rest
    j = pl.program_id(1)
    tm = x_ref.shape[0]

    @pl.when(j == 0)
    def _():
        xn = _rms(x_ref[...], g_ref[...]).astype(BF16)
        xn_ref[...] = xn
        gate = _dot(xn, wgate_ref[...])
        gate_ref[...] = gate
        if with_gate_t:
            gate_t_ref[...] = gate.T

    @pl.when(j < N_BF_BLOCKS)
    def _():
        pbf_ref[...] = _dot(xn_ref[...], w_ref[...]).astype(BF16)

    for k, ref in enumerate((fk_ref, fv_ref, bk_ref, bv_ref)):
        @pl.when(j == N_BF_BLOCKS + k)
        def _(ref=ref):
            y = _dot(xn_ref[...], w_ref[...])
            for h in range(N_HEADS):
                ref[pl.ds(h, tm, stride=N_HEADS), :] = y[:, h * HEAD_DIM:(h + 1) * HEAD_DIM]


def _inproj(x, g, w_all, w_gate, kv, *, layer, tm, with_gate_t):
    m, d = x.shape
    nb = N_BF_BLOCKS + N_F32_BLOCKS
    rows = pl.BlockSpec((None, N_HEADS * tm, HEAD_DIM), lambda i, j: (layer, i, 0))
    out_shape = [jax.ShapeDtypeStruct((m, N_BF_BLOCKS * W_MIX), BF16)]
    out_shape += [jax.ShapeDtypeStruct(a.shape, F32) for a in kv]
    out_shape += [jax.ShapeDtypeStruct((m, LANES), F32)]
    out_specs = [
        pl.BlockSpec((tm, W_MIX), lambda i, j: (i, jnp.minimum(j, N_BF_BLOCKS - 1))),
        rows, rows, rows, rows,
        pl.BlockSpec((tm, LANES), lambda i, j: (i, 0)),
    ]
    if with_gate_t:
        out_shape.append(jax.ShapeDtypeStruct((LANES, m), F32))
        out_specs.append(pl.BlockSpec((LANES, tm), lambda i, j: (0, i)))
    untouched = pl.BlockSpec(memory_space=pl.ANY)
    outs = pl.pallas_call(
        functools.partial(_inproj_kernel, with_gate_t=with_gate_t),
        grid=(m // tm, nb),
        in_specs=[
            pl.BlockSpec((tm, d), lambda i, j: (i, 0)),
            pl.BlockSpec((1, d), lambda i, j: (0, 0)),
            pl.BlockSpec((None, d, W_MIX), lambda i, j: (layer, 0, j)),
            pl.BlockSpec((None, d, LANES), lambda i, j: (layer, 0, 0)),
            untouched, untouched, untouched, untouched,
        ],
        out_specs=out_specs,
        out_shape=out_shape,
        input_output_aliases={4: 1, 5: 2, 6: 3, 7: 4},
        scratch_shapes=[pltpu.VMEM((tm, d), BF16)],
        compiler_params=_cparams("parallel", "arbitrary"),
        name="inproj",
    )(x, g, w_all, w_gate, *kv)
    return (outs[0], tuple(outs[1:5])) + tuple(outs[5:])


def _merge_kernel(x_ref, mo_ref, fo_ref, bo_ref, gm_ref, gf_ref, gb_ref, g_ref,
                  wm_ref, wf_ref, wb_ref, wo_ref, o_ref):
    merged = (_sigmoid(gm_ref[...].astype(F32)) * _dot(mo_ref[...], wm_ref[...])
              + _sigmoid(gf_ref[...].astype(F32)) * _dot(fo_ref[...], wf_ref[...])
              + _sigmoid(gb_ref[...].astype(F32)) * _dot(bo_ref[...], wb_ref[...]))
    y = _dot(merged.astype(BF16), wo_ref[...])
    o_ref[...] = x_ref[...] + _rms(y, g_ref[...])


def _merge_out(x, m_out, f_out, b_out, pbf, g, w_m, w_f, w_b, w_o, *, layer, tm):
    m, d = x.shape
    gate_blk = d // W_MIX

    def gate_spec(k):
        return pl.BlockSpec((tm, d), lambda i: (i, (6 + k * gate_blk) // gate_blk))

    row = pl.BlockSpec((tm, W_MIX), lambda i: (i, 0))
    wbr = pl.BlockSpec((None, W_MIX, d), lambda i: (layer, 0, 0))
    return pl.pallas_call(
        _merge_kernel,
        grid=(m // tm,),
        in_specs=[
            pl.BlockSpec((tm, d), lambda i: (i, 0)), row, row, row,
            gate_spec(0), gate_spec(1), gate_spec(2),
            pl.BlockSpec((1, d), lambda i: (0, 0)),
            wbr, wbr, wbr, pl.BlockSpec((None, d, d), lambda i: (layer, 0, 0)),
        ],
        out_specs=pl.BlockSpec((tm, d), lambda i: (i, 0)),
        out_shape=jax.ShapeDtypeStruct((m, d), F32),
        compiler_params=_cparams("parallel"),
        name="merge_out",
    )(x, m_out, f_out, b_out, pbf, pbf, pbf, g, w_m, w_f, w_b, w_o)


def _w_in_offsets(d_model):
    widths = (("m_q", W_MIX), ("m_k", W_MIX), ("m_v", W_MIX), ("m_o", W_MIX), ("m_i", N_HEADS), ("m_f", N_HEADS),
              ("f_q", W_MIX), ("f_k", W_MIX), ("f_v", W_MIX), ("f_f", N_HEADS),
              ("b_q", W_MIX), ("b_k", W_MIX), ("b_v", W_MIX),
              ("g_m", d_model), ("g_f", d_model), ("g_b", d_model))
    out, off = {}, 0
    for name, w in widths:
        out[name] = (off, off + w)
        off += w
    return out


def _prep_w_in(w_in):
    d = w_in.shape[-2]
    offs = _w_in_offsets(d)

    def cols(name):
        a, b = offs[name]
        return w_in[..., a:b]

    order = ("m_q", "m_k", "m_v", "m_o", "f_q", "b_q", "g_m", "g_f", "g_b", "f_k", "f_v", "b_k", "b_v")
    w_all = jnp.concatenate([cols(n) for n in order], axis=-1).astype(BF16)
    gate = jnp.concatenate([cols("m_i"), cols("m_f"), cols("f_f")], axis=-1)
    pad = [(0, 0)] * (gate.ndim - 1) + [(0, LANES - gate.shape[-1])]
    w_gate = jnp.pad(gate, pad).astype(BF16)
    return w_all, w_gate


def _gateprep_kernel(gate_ref, gate_t_ref, bias_ref, bias_t_ref,
                     act_ref, cl_ref, act_t_ref, cl_t_ref, kb_ref, qb_ref, cf_ref, *, chunk):
    t = gate_ref.shape[0]
    lane = lax.broadcasted_iota(jnp.int32, (1, LANES), 1)
    z = gate_ref[...] + bias_ref[...]
    act_ref[...] = jnp.where(lane < N_HEADS, z, _log_sigmoid(z))
    sub = lax.broadcasted_iota(jnp.int32, (LANES, 1), 0)
    zt = gate_t_ref[...] + bias_t_ref[...]
    act_t_ref[...] = jnp.where(sub < N_HEADS, zt, _log_sigmoid(zt))

    r = lax.broadcasted_iota(jnp.int32, (chunk, chunk), 0)
    c = lax.broadcasted_iota(jnp.int32, (chunk, chunk), 1)
    tril = jnp.where(c <= r, 1.0, 0.0).astype(BF16)
    triu = jnp.where(r <= c, 1.0, 0.0).astype(BF16)
    carry = jnp.zeros((1, LANES), F32)
    for ci in range(t // chunk):
        sl = slice(ci * chunk, (ci + 1) * chunk)
        loc = _dot3_r(tril, act_ref[sl, :])
        cl_ref[sl, :] = loc
        full = loc + carry
        cf_ref[sl, :] = full
        carry = full[chunk - 1:chunk, :]
        cl_t_ref[:, sl] = _dot3_l(act_t_ref[:, sl], triu)

    terms = _split3(cf_ref[...] * (HEAD_DIM ** 0.5))
    r = lax.broadcasted_iota(jnp.int32, (LANES, W_MIX), 0)
    c = lax.broadcasted_iota(jnp.int32, (LANES, W_MIX), 1)
    own = r == 2 * N_HEADS + c // HEAD_DIM
    lane = lax.broadcasted_iota(jnp.int32, (1, W_MIX), 1) % HEAD_DIM

    def place(first):
        return sum(_dot(x, jnp.where(own & (c % HEAD_DIM == first + j), 1.0, 0.0).astype(BF16))
                   for j, x in enumerate(terms))

    kb_ref[...] = jnp.where((lane >= 3) & (lane < 6), 1.0, -place(0)).astype(BF16)
    qb_ref[...] = jnp.where(lane < 3, 1.0, place(3)).astype(BF16)


def _gateprep(gate, gate_t, bias, bias_t, *, n_seq, chunk):
    t = gate.shape[0] // n_seq
    col = pl.BlockSpec((t, LANES), lambda b: (b, 0))
    row = pl.BlockSpec((LANES, t), lambda b: (0, b))
    col_s = jax.ShapeDtypeStruct(gate.shape, F32)
    row_s = jax.ShapeDtypeStruct(gate_t.shape, F32)
    dec = pl.BlockSpec((t, W_MIX), lambda b: (b, 0))
    dec_s = jax.ShapeDtypeStruct((gate.shape[0], W_MIX), BF16)
    return pl.pallas_call(
        functools.partial(_gateprep_kernel, chunk=chunk),
        grid=(n_seq,),
        in_specs=[col, row, pl.BlockSpec((1, LANES), lambda b: (0, 0)), pl.BlockSpec((LANES, 1), lambda b: (0, 0))],
        out_specs=[col, col, row, row, dec, dec],
        out_shape=[col_s, col_s, row_s, row_s, dec_s, dec_s],
        scratch_shapes=[pltpu.VMEM((t, LANES), F32)],
        compiler_params=_cparams("parallel"),
        name="gateprep",
    )(gate, gate_t, bias, bias_t)


def _mlstm_kernel(q_ref, k_ref, v_ref, o_ref, wq_ref, wk_ref, hn_ref, act_ref, cl_ref, act_t_ref, cl_t_ref,
                  out_ref, caug_ref, m_ref, cbuf, qs, kts, icb, bcb, cst, *, chunk):
    h = pl.program_id(1)
    t = q_ref.shape[0]
    reps = chunk // LANES

    def wide(a, n):
        return jnp.concatenate([a] * n, axis=1) if n > 1 else a

    def conv_silu(x_ref, w_ref):
        cbuf[0:8, :] = jnp.zeros((8, LANES), F32)
        cbuf[8:, :] = x_ref[...].astype(F32)
        y = w_ref[0:1, :] * cbuf[pl.ds(8 - (CONV_W - 1), t), :]
        for j in range(1, CONV_W):
            y = y + w_ref[j:j + 1, :] * cbuf[pl.ds(8 - (CONV_W - 1) + j, t), :]
        return y * _sigmoid(y)

    qs[...] = conv_silu(q_ref, wq_ref).astype(BF16)
    kts[...] = (conv_silu(k_ref, wk_ref) * (HEAD_DIM ** -0.5)).T.astype(BF16)

    rr = lax.broadcasted_iota(jnp.int32, (LANES, LANES), 0)
    icb[...] = _dot3_l(act_ref[...], jnp.where(rr == h, 1.0, 0.0).astype(BF16))
    bcb[...] = _dot3_l(cl_ref[...], jnp.where(rr == N_HEADS + h, 1.0, 0.0).astype(BF16))

    cst[...] = jnp.zeros_like(cst)
    tt = lax.broadcasted_iota(jnp.int32, (chunk, chunk), 0)
    ss = lax.broadcasted_iota(jnp.int32, (chunk, chunk), 1)
    ones = jnp.ones((chunk, LANES), BF16)

    def body(c, m_prev):
        r0 = pl.multiple_of(c * chunk, chunk)
        rows = pl.ds(r0, chunk)
        qc = qs[rows, :]
        ktc = kts[:, rows]
        v_aug = jnp.concatenate([v_ref[rows, :], ones], axis=1)
        i_r = act_t_ref[pl.ds(h, 1), rows]
        b_r = cl_t_ref[pl.ds(N_HEADS + h, 1), rows]
        i_c = icb[rows, :]
        b_c = bcb[rows, :]
        dmat = jnp.where(ss <= tt, wide(b_c, reps) + (i_r - b_r), -jnp.inf)
        inter = b_c + m_prev
        m_t = jnp.maximum(inter, jnp.max(dmat, axis=-1, keepdims=True))
        w_inter = jnp.exp(inter - m_t)
        s = (_dot(qc, ktc) * jnp.exp(dmat - wide(m_t, reps))).astype(BF16)
        c_prev = cst[...]
        num = wide(w_inter, 2) * _dot(qc, c_prev.astype(BF16)) + _dot(s, v_aug)
        hh = num[:, :LANES] / jnp.maximum(jnp.abs(num[:, LANES:]), jnp.exp(-m_t))
        hh = hh * lax.rsqrt(jnp.mean(hh * hh, axis=-1, keepdims=True) + EPS) * hn_ref[...]
        out_ref[rows, :] = (_sigmoid(o_ref[rows, :].astype(F32)) * hh).astype(BF16)

        b_last = b_r[:, chunk - 1:chunk]
        m_new = jnp.maximum(b_last + m_prev, jnp.max(b_last - b_r + i_r, axis=-1, keepdims=True))
        a_prev = jnp.exp(b_last + m_prev - m_new)
        wg_c = jnp.exp(b_last - b_c + i_c - m_new)
        wv = (wide(wg_c, 2) * v_aug.astype(F32)).astype(BF16)
        cst[...] = a_prev * c_prev + _dot(ktc, wv)
        return m_new

    m_fin = lax.fori_loop(0, t // chunk, body, jnp.full((1, 1), -jnp.inf, F32), unroll=MLSTM_UNROLL)
    caug_ref[...] = cst[...]
    m_ref[...] = jnp.broadcast_to(m_fin, m_ref.shape)


def _mlstm_prompt(pbf, w_conv, g_hn, act, cl, act_t, cl_t, *, n_seq, chunk):
    bt = pbf.shape[0]
    t = bt // n_seq

    def head_cols(blk):
        return pl.BlockSpec((t, HEAD_DIM), lambda b, h: (b, blk * N_HEADS + h))

    col = pl.BlockSpec((t, LANES), lambda b, h: (b, 0))
    row = pl.BlockSpec((LANES, t), lambda b, h: (0, b))
    return pl.pallas_call(
        functools.partial(_mlstm_kernel, chunk=chunk),
        grid=(n_seq, N_HEADS),
        in_specs=[
            head_cols(0), head_cols(1), head_cols(2), head_cols(3),
            pl.BlockSpec((CONV_W, HEAD_DIM), lambda b, h: (0, h)),
            pl.BlockSpec((CONV_W, HEAD_DIM), lambda b, h: (0, N_HEADS + h)),
            pl.BlockSpec((1, HEAD_DIM), lambda b, h: (0, h)),
            col, col, row, row,
        ],
        out_specs=[
            pl.BlockSpec((t, HEAD_DIM), lambda b, h: (b, h)),
            pl.BlockSpec((None, None, HEAD_DIM, 2 * LANES), lambda b, h: (b, h, 0, 0)),
            pl.BlockSpec((None, None, 8, LANES), lambda b, h: (b, h, 0, 0)),
        ],
        out_shape=[
            jax.ShapeDtypeStruct((bt, W_MIX), BF16),
            jax.ShapeDtypeStruct((n_seq, N_HEADS, HEAD_DIM, 2 * LANES), F32),
            jax.ShapeDtypeStruct((n_seq, N_HEADS, 8, LANES), F32),
        ],
        scratch_shapes=[
            pltpu.VMEM((t + 8, LANES), F32), pltpu.VMEM((t, HEAD_DIM), BF16), pltpu.VMEM((HEAD_DIM, t), BF16),
            pltpu.VMEM((t, LANES), F32), pltpu.VMEM((t, LANES), F32), pltpu.VMEM((HEAD_DIM, 2 * LANES), F32),
        ],
        compiler_params=_cparams("parallel", "arbitrary"),
        name="mlstm_prompt",
    )(pbf, pbf, pbf, pbf, w_conv, w_conv, g_hn, act, cl, act_t, cl_t)


def _attn_kernel(*refs, moba, n_slots):
    n_in = 3 if moba else 5
    q_ref, k_ref, v_ref = refs[:3]
    o_ref = refs[n_in]
    kaug, vt = refs[n_in + 1:n_in + 3]
    per_slot = refs[n_in + 3:n_in + 3 + 2 * n_slots]
    if moba:
        kmean = refs[n_in + 3 + 2 * n_slots]
        sels = refs[n_in + 4 + 2 * n_slots:]
    else:
        kb_ref, qb_ref = refs[3:5]
        sels = (None,) * n_slots
    slots = [(per_slot[2 * i], per_slot[2 * i + 1], sels[i]) for i in range(n_slots)]
    jp = pl.program_id(2)
    t = kaug.shape[0]
    blk = slots[0][0].shape[0]
    n_blk = t // blk
    scale = HEAD_DIM ** -0.5
    c_exp = scale * math.log2(math.e)

    @pl.when(jp == 0)
    def _():
        own_rows = pl.ds(pl.program_id(1), t, stride=N_HEADS)
        k = k_ref[own_rows, :]
        kaug[:, :HEAD_DIM] = k.astype(BF16)
        vt[...] = v_ref[own_rows, :].T.astype(BF16)
        if moba:
            lane = lax.broadcasted_iota(jnp.int32, (t, LANES), 1)
            kblk = lax.broadcasted_iota(jnp.int32, (t, LANES), 0) // blk
            kaug[:, HEAD_DIM:] = jnp.where(lane == kblk, 1.0, 0.0).astype(BF16)
            kmean[...] = jnp.zeros_like(kmean)
            for n in range(n_blk):
                kmean[n:n + 1, :] = jnp.mean(k[n * blk:(n + 1) * blk, :], axis=0, keepdims=True)
        else:
            kaug[:, HEAD_DIM:] = kb_ref[...]

    ks = lax.broadcasted_iota(jnp.int32, (blk, blk), 0)
    qs = lax.broadcasted_iota(jnp.int32, (blk, blk), 1)

    def attend(qi, slot):
        qaug, pbuf, selb = slots[slot]
        rows = slice(qi * blk, (qi + 1) * blk)
        qb = q_ref[rows, :]
        qaug[:, :HEAD_DIM] = qb
        if moba:
            nbp = kmean.shape[0]
            row = lax.broadcasted_iota(jnp.int32, (nbp, blk), 0)
            km_hi, km_mid, km_lo = _split3(kmean[...])
            gs = _dot_nt(km_hi, qb) + _dot_nt(km_mid, qb) + _dot_nt(km_lo, qb)
            gm = jnp.where(row < qi, gs, -jnp.inf)
            selb[...] = jnp.zeros_like(selb)
            for n in range(qi):
                g_n = gm[n:n + 1, :]
                beats = (gm > g_n) | ((gm == g_n) & (row < n))
                cnt = jnp.sum(jnp.where(beats, 1.0, 0.0), axis=0, keepdims=True)
                selb[n:n + 1, :] = jnp.where(cnt < MOBA_TOPK, 0.0, NEG / scale)
            qaug[:, HEAD_DIM:] = selb[...].T.astype(BF16)
        else:
            qaug[:, HEAD_DIM:] = qb_ref[rows, :]
        qa = qaug[...]

        def raw(c):
            s = _dot_nt(kaug[c * blk:(c + 1) * blk, :], qa)
            return jnp.where(ks <= qs, s, NEG) if c == qi else s

        m = jnp.max(raw(qi), axis=0, keepdims=True)
        for c in range(qi):
            m = jnp.maximum(m, jnp.max(raw(c), axis=0, keepdims=True))
        l = jnp.zeros((1, blk), F32)
        for c in range(qi + 1):
            p = jnp.exp2((raw(c) - m) * c_exp)
            l = l + jnp.sum(p, axis=0, keepdims=True)
            pbuf[c * blk:(c + 1) * blk, :] = p.astype(BF16)
        nk = (qi + 1) * blk
        acc = _dot(vt[:, 0:nk], pbuf[0:nk, :])
        o_ref[rows, :] = (acc / l).T.astype(BF16)

    n_pairs = -(-n_blk // 2)
    per_step = n_slots // 2

    def step(s):
        for r in range(per_step):
            j = s * per_step + r
            if j < n_pairs:
                attend(j, 2 * r)
                if n_blk - 1 - j != j:
                    attend(n_blk - 1 - j, 2 * r + 1)

    for s in range(-(-n_pairs // per_step)):
        pl.when(jp == s)(functools.partial(step, s))


def _attn_prompt(pbf, k, v, kbias, qbias, *, layer, n_seq, q_block, moba):
    bt = pbf.shape[0]
    t = bt // n_seq
    blk = MOBA_BLOCK
    nq = t // blk
    assert nq <= LANES
    head = pl.BlockSpec((t, HEAD_DIM), lambda b, h, j: (b, h))
    kv = pl.BlockSpec((None, N_HEADS * t, HEAD_DIM), lambda b, h, j: (layer, b, 0))
    in_specs = [pl.BlockSpec((t, HEAD_DIM), lambda b, h, j: (b, q_block * N_HEADS + h)), kv, kv]
    args = [pbf, k, v]
    n_slots = 2 * ATTN_PAIRS_PER_STEP
    slot = [pltpu.VMEM((blk, 2 * HEAD_DIM), BF16), pltpu.VMEM((t, blk), BF16)]
    scratch = [pltpu.VMEM((t, 2 * HEAD_DIM), BF16), pltpu.VMEM((HEAD_DIM, t), BF16)] + slot * n_slots
    if moba:
        nbp = -(-nq // 8) * 8
        scratch += [pltpu.VMEM((nbp, HEAD_DIM), F32)] + [pltpu.VMEM((LANES, blk), F32)] * n_slots
    else:
        in_specs += [head, head]
        args += [kbias, qbias]
    return pl.pallas_call(
        functools.partial(_attn_kernel, moba=moba, n_slots=n_slots),
        grid=(n_seq, N_HEADS, -(-(-(-nq // 2)) // ATTN_PAIRS_PER_STEP)),
        in_specs=in_specs,
        out_specs=head,
        out_shape=jax.ShapeDtypeStruct((bt, W_MIX), BF16),
        scratch_shapes=scratch,
        compiler_params=_cparams("parallel", "arbitrary", "arbitrary"),
        name="moba_prompt" if moba else "fox_prompt",
    )(*args)


def _mlstm_step_kernel(p_ref, gate_ref, bias_ref, conv_ref, wc_ref, hn_ref, c0_ref, n0_ref, m0_ref,
                       out_ref, c_ref, n_ref, m_ref):
    x = p_ref[...].astype(F32)
    qk = wc_ref[CONV_W - 1:CONV_W, :] * x[:, :2 * W_MIX]
    for j in range(CONV_W - 1):
        qk = qk + wc_ref[j:j + 1, :] * conv_ref[j:j + 1, :]
    qk = qk * _sigmoid(qk)
    q = qk[:, :W_MIX]
    k = qk[:, W_MIX:] * (HEAD_DIM ** -0.5)
    v = x[:, 2 * W_MIX:3 * W_MIX]
    og = _sigmoid(x[:, 3 * W_MIX:])
    z = gate_ref[...] + bias_ref[...]
    logf = _log_sigmoid(z)
    lane = lax.broadcasted_iota(jnp.int32, (1, LANES), 1)
    m_row = jnp.zeros((1, LANES), F32)
    outs = []
    for h in range(N_HEADS):
        hs = slice(h * HEAD_DIM, (h + 1) * HEAD_DIM)
        i_h = z[:, h:h + 1]
        f_h = logf[:, N_HEADS + h:N_HEADS + h + 1]
        m0 = m0_ref[:, h:h + 1]
        m_new = jnp.maximum(f_h + m0, i_h)
        a = jnp.exp(f_h + m0 - m_new)
        wgt = jnp.exp(i_h - m_new)
        k_col = jnp.broadcast_to(k[:, hs], (HEAD_DIM, HEAD_DIM)).T
        q_col = jnp.broadcast_to(q[:, hs], (HEAD_DIM, HEAD_DIM)).T
        c_new = a * c0_ref[h] + (wgt * k_col) * v[:, hs]
        n_new = a * n0_ref[h:h + 1, :] + wgt * k[:, hs]
        c_ref[h] = c_new
        n_ref[h:h + 1, :] = n_new
        num = jnp.sum(q_col * c_new, axis=0, keepdims=True)
        den = jnp.sum(q[:, hs] * n_new, axis=-1, keepdims=True)
        hh = num / jnp.maximum(jnp.abs(den), jnp.exp(-m_new))
        hh = hh * lax.rsqrt(jnp.mean(hh * hh, axis=-1, keepdims=True) + EPS) * hn_ref[:, hs]
        outs.append(og[:, hs] * hh)
        m_row = jnp.where(lane == h, m_new, m_row)
    out_ref[...] = jnp.concatenate(outs, axis=1).astype(BF16)
    m_ref[...] = m_row


def _mlstm_step(pbf3, gate3, bias, conv_prev, w_conv, g_hn, c0, n0, m0):
    bs = pbf3.shape[0]
    return pl.pallas_call(
        _mlstm_step_kernel,
        grid=(bs,),
        in_specs=[
            pl.BlockSpec((None, 1, 4 * W_MIX), lambda b: (b, 0, 0)),
            pl.BlockSpec((None, 1, LANES), lambda b: (b, 0, 0)),
            pl.BlockSpec((1, LANES), lambda b: (0, 0)),
            pl.BlockSpec((None, CONV_W - 1, 2 * W_MIX), lambda b: (b, 0, 0)),
            pl.BlockSpec((CONV_W, 2 * W_MIX), lambda b: (0, 0)),
            pl.BlockSpec((1, W_MIX), lambda b: (0, 0)),
            pl.BlockSpec((None, N_HEADS, HEAD_DIM, HEAD_DIM), lambda b: (b, 0, 0, 0)),
            pl.BlockSpec((None, N_HEADS, HEAD_DIM), lambda b: (b, 0, 0)),
            pl.BlockSpec((None, 1, N_HEADS), lambda b: (b, 0, 0)),
        ],
        out_specs=[
            pl.BlockSpec((None, 1, W_MIX), lambda b: (b, 0, 0)),
            pl.BlockSpec((None, N_HEADS, HEAD_DIM, HEAD_DIM), lambda b: (b, 0, 0, 0)),
            pl.BlockSpec((None, N_HEADS, HEAD_DIM), lambda b: (b, 0, 0)),
            pl.BlockSpec((None, 1, LANES), lambda b: (b, 0, 0)),
        ],
        out_shape=[
            jax.ShapeDtypeStruct((bs, 1, W_MIX), BF16),
            jax.ShapeDtypeStruct(c0.shape, F32),
            jax.ShapeDtypeStruct(n0.shape, F32),
            jax.ShapeDtypeStruct((bs, 1, LANES), F32),
        ],
        compiler_params=_cparams("parallel"),
        name="mlstm_step",
    )(pbf3, gate3, bias, conv_prev, w_conv, g_hn, c0, n0, m0)


PAGE_ROWS = PAGE * N_HEADS


def _pad_rows8(a):
    return jnp.concatenate([a, jnp.zeros_like(a)], axis=0)


def _decode_kernel(pt_ref, *refs, group, n_pages):
    del pt_ref
    pages = refs[:4 * group]
    (fq_ref, bq_ref, knew_ref, vnew_ref, gate_ref, bias_ref,
     out_ref, lf_ref, top_ref, qf, qb, m_s, l_s, acc, carry, ksum, msuf, mtot) = refs[4 * group:]
    first = (pl.program_id(0) == 0) & (pl.program_id(1) == 0)
    g = pl.program_id(1)
    n_blocks = n_pages // 2
    scale = HEAD_DIM ** -0.5

    @pl.when(first)
    def _():
        src = lax.broadcasted_iota(jnp.int32, (PAGE_ROWS, PAGE_ROWS), 0)
        dst = lax.broadcasted_iota(jnp.int32, (PAGE_ROWS, PAGE_ROWS), 1)
        same_head = (src & (N_HEADS - 1)) == (dst & (N_HEADS - 1))
        mtot[...] = jnp.where(same_head, 1.0, 0.0).astype(BF16)
        msuf[...] = jnp.where(same_head & (src > dst), 1.0, 0.0).astype(BF16)

    @pl.when(g == 0)
    def _():
        qf[...] = _pad_rows8(fq_ref[...].astype(F32))
        qb[...] = _pad_rows8(bq_ref[...].astype(F32))
        m_s[...] = jnp.full(m_s.shape, NEG, F32)
        l_s[...] = jnp.zeros_like(l_s)
        acc[...] = jnp.zeros_like(acc)
        lf_new = _log_sigmoid(gate_ref[...] + bias_ref[...])
        lf_ref[...] = lf_new
        r = lax.broadcasted_iota(jnp.int32, (LANES, PAGE_ROWS), 0)
        c = lax.broadcasted_iota(jnp.int32, (LANES, PAGE_ROWS), 1)
        spread = jnp.where(r == 2 * N_HEADS + (c & (N_HEADS - 1)), 1.0, 0.0).astype(BF16)
        carry[...] = _dot3_l(jnp.broadcast_to(lf_new, (8, LANES)), spread)[0:1]
        ksum[...] = jnp.zeros_like(ksum)

    row = lax.broadcasted_iota(jnp.int32, (8, PAGE_ROWS), 0)
    lane = lax.broadcasted_iota(jnp.int32, (8, PAGE_ROWS), 1)
    own_head = (lane & (N_HEADS - 1)) == row
    wide = lambda a: jnp.concatenate([a] * N_HEADS, axis=1)

    lf_rows = jnp.concatenate([pages[4 * i + 3][...] for i in range(group)], axis=0)
    suf = _dot3_l(lf_rows, msuf[...])
    tot = _dot3_l(lf_rows, mtot[...])
    run = carry[...]
    q_f = qf[...].astype(BF16)
    s_all = []
    for i in range(group):
        s = _dot_nt(q_f, pages[4 * i][...].astype(BF16)) * scale + (suf[i:i + 1] + run)
        s_all.append(jnp.where(own_head, s, NEG))
        run = run + tot[i:i + 1]
    carry[...] = run
    m_old = m_s[...]
    m_new = m_old
    for s in s_all:
        m_new = jnp.maximum(m_new, jnp.max(s, axis=-1, keepdims=True))
    alpha = jnp.exp(m_old - m_new)
    l_new = alpha * l_s[...]
    a_new = alpha * acc[...]
    m_wide = wide(m_new)
    for i, s in enumerate(s_all):
        p = jnp.exp(s - m_wide)
        l_new = l_new + jnp.sum(p, axis=-1, keepdims=True)
        a_new = a_new + _dot(p.astype(BF16), pages[4 * i + 1][...].astype(BF16))
    m_s[...] = m_new
    l_s[...] = l_new
    acc[...] = a_new

    sub8 = lax.broadcasted_iota(jnp.int32, (8, HEAD_DIM), 0)
    for i in range(group):
        vs = jnp.sum(pages[4 * i + 2][...].reshape(PAGE_ROWS // 8, 8, HEAD_DIM), axis=0)
        per_head = jnp.where(sub8 < N_HEADS, vs + pltpu.roll(vs, N_HEADS, 0), 0.0)
        r0 = pl.multiple_of((n_blocks - 1 - (g * group + i) // 2) * 8, 8)
        ksum[pl.ds(r0, 8), :] = ksum[pl.ds(r0, 8), :] + per_head

    @pl.when(g == pl.num_programs(1) - 1)
    def _():
        s_new = jnp.sum(qf[...] * _pad_rows8(knew_ref[...]), axis=-1, keepdims=True) * scale
        m_old = m_s[...]
        m_fin = jnp.maximum(m_old, s_new)
        alpha = jnp.exp(m_old - m_fin)
        p_new = jnp.exp(s_new - m_fin)
        l_fin = alpha * l_s[...] + p_new
        o = (alpha * acc[...] + p_new * _pad_rows8(vnew_ref[...])) / l_fin
        out_ref[...] = o[0:N_HEADS].astype(BF16)
        km_hi, km_mid, km_lo = _split3(ksum[...] * (1.0 / MOBA_BLOCK))
        q_b = qb[...].astype(BF16)
        gs = _dot_nt(q_b, km_hi) + _dot_nt(q_b, km_mid) + _dot_nt(q_b, km_lo)
        grow = lax.broadcasted_iota(jnp.int32, gs.shape, 0)
        glane = lax.broadcasted_iota(jnp.int32, gs.shape, 1)
        gs = jnp.where((glane & 7) == grow, gs, -jnp.inf)
        glane_f = glane.astype(F32)
        tlane = lax.broadcasted_iota(jnp.int32, (8, LANES), 1)
        top = jnp.zeros((8, LANES), F32)
        for slot in range(MOBA_TOPK):
            best = jnp.max(gs, axis=-1, keepdims=True)
            idx = jnp.min(jnp.where(gs == best, glane_f, float(8 * n_blocks)), axis=-1, keepdims=True)
            top = jnp.where(tlane == slot, idx, top)
            gs = jnp.where(glane_f == idx, -jnp.inf, gs)
        top_ref[...] = jnp.right_shift(top.astype(jnp.int32), 3)


def _decode_attn(page_table, cfk, cfv, cbk, lfc, fq4, bq4, fk4, fv4, gate3, bias, *, layer, group):
    bs, n_pages = page_table.shape
    assert n_pages % group == 0 and group % 2 == 0 and n_pages // 2 >= MOBA_TOPK

    def page_spec(i, shape):
        def imap(b, g, pt):
            return (layer, pt[b * n_pages + (n_pages - 1 - (g * group + i))], 0, 0)
        return pl.BlockSpec((None, None) + shape, imap)

    in_specs, args = [], []
    for i in range(group):
        in_specs += [page_spec(i, (PAGE_ROWS, HEAD_DIM)), page_spec(i, (PAGE_ROWS, HEAD_DIM)),
                     page_spec(i, (PAGE_ROWS, HEAD_DIM)), page_spec(i, (1, PAGE_ROWS))]
        args += [cfk, cfv, cbk, lfc]
    per_head = pl.BlockSpec((None, N_HEADS, HEAD_DIM), lambda b, g, pt: (b, 0, 0))
    in_specs += [per_head, per_head, per_head, per_head,
                 pl.BlockSpec((None, 1, LANES), lambda b, g, pt: (b, 0, 0)),
                 pl.BlockSpec((1, LANES), lambda b, g, pt: (0, 0))]
    args += [fq4, bq4, fk4, fv4, gate3, bias]
    small = pltpu.VMEM((8, LANES), F32)
    return pl.pallas_call(
        functools.partial(_decode_kernel, group=group, n_pages=n_pages),
        grid_spec=pltpu.PrefetchScalarGridSpec(
            num_scalar_prefetch=1,
            grid=(bs, n_pages // group),
            in_specs=in_specs,
            out_specs=[
                per_head,
                pl.BlockSpec((None, 1, LANES), lambda b, g, pt: (b, 0, 0)),
                pl.BlockSpec((None, 8, LANES), lambda b, g, pt: (b, 0, 0)),
            ],
            scratch_shapes=[
                small, small, small, small, small,
                pltpu.VMEM((1, PAGE_ROWS), F32),
                pltpu.VMEM((8 * (n_pages // 2), HEAD_DIM), F32),
                pltpu.VMEM((PAGE_ROWS, PAGE_ROWS), BF16), pltpu.VMEM((PAGE_ROWS, PAGE_ROWS), BF16),
            ],
        ),
        out_shape=[
            jax.ShapeDtypeStruct((bs, N_HEADS, HEAD_DIM), BF16),
            jax.ShapeDtypeStruct((bs, 1, LANES), F32),
            jax.ShapeDtypeStruct((bs, 8, LANES), jnp.int32),
        ],
        compiler_params=_cparams("arbitrary", "arbitrary"),
        name="decode_attn",
    )(page_table.reshape(-1), *args)


MOBA_PAGES = 2 * MOBA_TOPK


def _moba_decode_kernel(pg_ref, k_hbm, v_hbm, q_ref, knew_ref, vnew_ref, out_ref, kbuf, vbuf, sem, *, layer):
    b = pl.program_id(0)
    slot = b % 2
    scale = HEAD_DIM ** -0.5

    def copies(seq, to_slot):
        out = []
        for h in range(N_HEADS):
            for j in range(MOBA_PAGES):
                i = h * MOBA_PAGES + j
                page = pg_ref[seq * N_HEADS * MOBA_PAGES + i]
                out.append(pltpu.make_async_copy(k_hbm.at[layer, page, :, h, :], kbuf.at[to_slot, i], sem.at[to_slot]))
                out.append(pltpu.make_async_copy(v_hbm.at[layer, page, :, h, :], vbuf.at[to_slot, i], sem.at[to_slot]))
        return out

    @pl.when(b == 0)
    def _():
        for c in copies(0, 0):
            c.start()

    @pl.when(b + 1 < pl.num_programs(0))
    def _():
        for c in copies(b + 1, 1 - slot):
            c.start()

    for c in copies(b, slot):
        c.wait()

    q = q_ref[...]
    rows = []
    for h in range(N_HEADS):
        q_h = q[h:h + 1, :]
        q8 = jnp.broadcast_to(q_h, (8, HEAD_DIM)).astype(BF16)
        pages = [h * MOBA_PAGES + i for i in range(MOBA_PAGES)]
        s = [_dot_nt(q8, kbuf[slot, i].astype(BF16))[0:1] * scale for i in pages]
        s_new = jnp.sum(q_h * knew_ref[h:h + 1, :], axis=-1, keepdims=True) * scale
        m = s_new
        for si in s:
            m = jnp.maximum(m, jnp.max(si, axis=-1, keepdims=True))
        p_new = jnp.exp(s_new - m)
        l = p_new
        o = p_new * vnew_ref[h:h + 1, :]
        for si, i in zip(s, pages):
            p = jnp.exp(si - m)
            l = l + jnp.sum(p, axis=-1, keepdims=True)
            o = o + _dot(jnp.broadcast_to(p, (8, PAGE)).astype(BF16), vbuf[slot, i].astype(BF16))[0:1]
        rows.append(o / l)
    out_ref[...] = jnp.concatenate(rows, axis=0).astype(BF16)


def _moba_decode(pages, cache_k, cache_v, bq, bk, bv, *, layer):
    bs = pages.shape[0]
    per_seq = pl.BlockSpec((None, N_HEADS, HEAD_DIM), lambda b, pg: (b, 0, 0))
    hbm = pl.BlockSpec(memory_space=pl.ANY)
    buf = pltpu.VMEM((2, N_HEADS * MOBA_PAGES, PAGE, HEAD_DIM), F32)
    return pl.pallas_call(
        functools.partial(_moba_decode_kernel, layer=layer),
        grid_spec=pltpu.PrefetchScalarGridSpec(
            num_scalar_prefetch=1,
            grid=(bs,),
            in_specs=[hbm, hbm, per_seq, per_seq, per_seq],
            out_specs=per_seq,
            scratch_shapes=[buf, buf, pltpu.SemaphoreType.DMA((2,))],
        ),
        out_shape=jax.ShapeDtypeStruct((bs, N_HEADS, HEAD_DIM), BF16),
        compiler_params=_cparams("arbitrary"),
        name="moba_decode",
    )(pages.reshape(-1), cache_k, cache_v, bq, bk, bv)


ROW_TILE = 1024
MERGE_TILE = 512
FF_TILE = 256
MLSTM_CHUNK = 256
DECODE_GROUP = 16
ATTN_PAIRS_PER_STEP = 4
MLSTM_UNROLL = 4


def kernel(x_prompt, x_sample, cache_fox_k, cache_fox_v, cache_fox_logf, cache_moba_k, cache_moba_v,
           state_mlstm_C, state_mlstm_n, state_mlstm_m, state_mlstm_conv, page_table,
           norm_g, w_ffn1_gu, w_ffn1_d, w_ffn2_gu, w_ffn2_d, w_in, b_mlstm_i, b_mlstm_f, b_fox_f,
           w_conv, g_headnorm, w_br_m, w_br_f, w_br_b, w_out):
    n_seq, t, d = x_prompt.shape
    bs, dec_t, _ = x_sample.shape
    depth, n_pool = cache_fox_k.shape[:2]
    assert dec_t == 1 and t % MOBA_BLOCK == 0 and cache_fox_k.shape[2:] == (PAGE, N_HEADS, HEAD_DIM)
    assert page_table.shape[1] % DECODE_GROUP == 0
    chunk = math.gcd(t, MLSTM_CHUNK)
    tm = math.gcd(n_seq * t, ROW_TILE)
    tm_merge = math.gcd(n_seq * t, MERGE_TILE)
    tf = math.gcd(w_ffn1_d.shape[1], FF_TILE)

    w_all, w_gate = _prep_w_in(w_in)
    wgu1, wd1, wgu2, wd2 = (w.astype(BF16) for w in (w_ffn1_gu, w_ffn1_d, w_ffn2_gu, w_ffn2_d))
    wbm, wbf, wbb, wo = (w.astype(BF16) for w in (w_br_m, w_br_f, w_br_b, w_out))
    bias = jnp.pad(jnp.concatenate([b_mlstm_i, b_mlstm_f, b_fox_f], axis=-1),
                   ((0, 0), (0, LANES - 3 * N_HEADS)))[:, None, :]
    bias_t = jnp.swapaxes(bias, 1, 2)
    paged = lambda c: c.reshape(depth, n_pool, PAGE_ROWS, HEAD_DIM)
    cfk, cfv, cbk = paged(cache_fox_k), paged(cache_fox_v), paged(cache_moba_k)
    lfc = cache_fox_logf.reshape(depth, n_pool, 1, PAGE_ROWS)
    hn = g_headnorm[:, None, :]

    xp = x_prompt.reshape(n_seq * t, d)
    xs = x_sample.reshape(bs, d)
    new_p, new_s = [], []
    kv_p = tuple(jnp.zeros((depth, N_HEADS * n_seq * t, HEAD_DIM), F32) for _ in range(4))
    kv_s = tuple(jnp.zeros((depth, N_HEADS * bs, HEAD_DIM), F32) for _ in range(4))
    for l in range(depth):
        g = [norm_g[l, i][None, :] for i in range(6)]
        xp = _ffn_block(xp, g[0], g[1], wgu1, wd1, layer=l, tm=tm, tf=tf)
        pbf, kv_p, gate, gate_t = _inproj(xp, g[2], w_all, w_gate, kv_p, layer=l, tm=tm, with_gate_t=True)
        act, cl, act_t, cl_t, kbias, qbias = _gateprep(gate, gate_t, bias[l], bias_t[l], n_seq=n_seq, chunk=chunk)
        m_out, caug, m_fin = _mlstm_prompt(pbf, w_conv[l], hn[l], act, cl, act_t, cl_t, n_seq=n_seq, chunk=chunk)
        f_out = _attn_prompt(pbf, kv_p[0], kv_p[1], kbias, qbias, layer=l, n_seq=n_seq, q_block=4, moba=False)
        b_out = _attn_prompt(pbf, kv_p[2], kv_p[3], None, None, layer=l, n_seq=n_seq, q_block=5, moba=True)
        xp = _merge_out(xp, m_out, f_out, b_out, pbf, g[3], wbm, wbf, wbb, wo, layer=l, tm=tm_merge)
        xp = _ffn_block(xp, g[4], g[5], wgu2, wd2, layer=l, tm=tm, tf=tf)
        conv_p = pbf.reshape(n_seq, t, -1)[:, t - (CONV_W - 1):, :2 * W_MIX].astype(F32)
        new_p.append((act[:, 2 * N_HEADS:3 * N_HEADS].reshape(n_seq, t, N_HEADS),
                      caug[..., :HEAD_DIM], caug[..., HEAD_DIM], m_fin[:, :, 0, 0], conv_p))
        xs = _ffn_block(xs, g[0], g[1], wgu1, wd1, layer=l, tm=bs, tf=tf)
        pbf_s, kv_s, gate_s = _inproj(xs, g[2], w_all, w_gate, kv_s, layer=l, tm=bs, with_gate_t=False)
        row3 = lambda a: a.reshape(bs, 1, -1)
        pbf3, gate3 = row3(pbf_s), row3(gate_s)
        m_out_s, c_s, n_s, m_s = _mlstm_step(pbf3, gate3, bias[l], state_mlstm_conv[l], w_conv[l], hn[l],
                                             state_mlstm_C[l], state_mlstm_n[l], state_mlstm_m[l][:, None, :])
        h3 = lambda a: a.reshape(bs, N_HEADS, HEAD_DIM)
        fq_s, bq_s = pbf_s[:, 4 * W_MIX:5 * W_MIX], pbf_s[:, 5 * W_MIX:6 * W_MIX]
        f_out_s, lf_s, top = _decode_attn(page_table, cfk, cfv, cbk, lfc, h3(fq_s), h3(bq_s), h3(kv_s[0][l]), h3(kv_s[1][l]),
                                          gate3, bias[l], layer=l, group=DECODE_GROUP)
        blk_pages = (2 * top[:, :N_HEADS, :MOBA_TOPK, None] + jnp.arange(2)).reshape(bs, N_HEADS * MOBA_PAGES)
        b_out_s = _moba_decode(jnp.take_along_axis(page_table, blk_pages, axis=1), cache_moba_k, cache_moba_v,
                               h3(bq_s).astype(F32), h3(kv_s[2][l]), h3(kv_s[3][l]), layer=l)
        flat = lambda a: a.reshape(bs, -1)
        xs = _merge_out(xs, flat(m_out_s), flat(f_out_s), flat(b_out_s), pbf_s, g[3], wbm, wbf, wbb, wo, layer=l, tm=bs)
        xs = _ffn_block(xs, g[4], g[5], wgu2, wd2, layer=l, tm=bs, tf=tf)
        conv_s = jnp.concatenate([state_mlstm_conv[l][:, 1:], pbf_s[:, None, :2 * W_MIX].astype(F32)], axis=1)
        new_s.append((lf_s[:, :, 2 * N_HEADS:3 * N_HEADS], c_s, n_s, m_s[:, 0, :N_HEADS], conv_s))
    logf_p, c_p, n_p, m_p, conv_p = [jnp.stack(z) for z in zip(*new_p)]
    logf_s, c_s, n_s, m_s, conv_s = [jnp.stack(z) for z in zip(*new_s)]
    kv_p = [a.reshape(depth, n_seq, t, N_HEADS, HEAD_DIM) for a in kv_p]
    kv_s = [a.reshape(depth, bs, 1, N_HEADS, HEAD_DIM) for a in kv_s]
    return (xp.reshape(n_seq, t, d), xs.reshape(bs, 1, d),
            kv_p[0], kv_p[1], logf_p, kv_p[2], kv_p[3], c_p, n_p, m_p, conv_p,
            kv_s[0], kv_s[1], logf_s, kv_s[2], kv_s[3], c_s, n_s, m_s, conv_s)
```

```python
import functools
import math

import jax
import jax.numpy as jnp
from jax import lax
from jax.experimental import pallas as pl
from jax.experimental.pallas import tpu as pltpu

F32 = jnp.float32
BF16 = jnp.bfloat16
EPS = 1e-6
NEG = -1e30

HEAD_DIM = 128
N_HEADS = 4
W_MIX = N_HEADS * HEAD_DIM
CONV_W = 4
MOBA_BLOCK = 256
MOBA_TOPK = 3
PAGE = 128
LANES = 128
VMEM_LIMIT = 52 * 1024 * 1024


def _cparams(*sem):
    return pltpu.CompilerParams(dimension_semantics=sem, vmem_limit_bytes=VMEM_LIMIT)


def _rms(x, g):
    return x * lax.rsqrt(jnp.mean(x * x, axis=-1, keepdims=True) + EPS) * g


def _split3(a):
    hi = a.astype(BF16)
    r = a - hi.astype(F32)
    mid = r.astype(BF16)
    lo = (r - mid.astype(F32)).astype(BF16)
    return hi, mid, lo


def _dot(a, b):
    return jnp.dot(a, b, preferred_element_type=F32)


def _dot_nt(a, b):
    return lax.dot_general(a, b, (((1,), (1,)), ((), ())), preferred_element_type=F32)


def _dot3_l(a_f32, b_bf16):
    hi, mid, lo = _split3(a_f32)
    return _dot(hi, b_bf16) + _dot(mid, b_bf16) + _dot(lo, b_bf16)


def _dot3_r(a_bf16, b_f32):
    hi, mid, lo = _split3(b_f32)
    return _dot(a_bf16, hi) + _dot(a_bf16, mid) + _dot(a_bf16, lo)


def _log_sigmoid(z):
    return jnp.minimum(z, 0.0) - jnp.log1p(jnp.exp(-jnp.abs(z)))


def _sigmoid(z):
    return 1.0 / (1.0 + jnp.exp(-z))


def _ffn_kernel(x_ref, ga_ref, gb_ref, wg_ref, wu_ref, wd_ref, o_ref, xn_ref, acc_ref):
    j = pl.program_id(1)

    @pl.when(j == 0)
    def _():
        xn_ref[...] = _rms(x_ref[...], ga_ref[...]).astype(BF16)
        acc_ref[...] = jnp.zeros_like(acc_ref)

    xn = xn_ref[...]
    g = _dot(xn, wg_ref[...])
    u = _dot(xn, wu_ref[...])
    h = (g * _sigmoid(g) * u).astype(BF16)
    acc_ref[...] += _dot(h, wd_ref[...])

    @pl.when(j == pl.num_programs(1) - 1)
    def _():
        o_ref[...] = x_ref[...] + 0.5 * _rms(acc_ref[...], gb_ref[...])


def _ffn_block(x, ga, gb, w_gu, w_d, *, layer, tm, tf):
    m, d = x.shape
    f = w_d.shape[1]
    nj = f // tf
    return pl.pallas_call(
        _ffn_kernel,
        grid=(m // tm, nj),
        in_specs=[
            pl.BlockSpec((tm, d), lambda i, j: (i, 0)),
            pl.BlockSpec((1, d), lambda i, j: (0, 0)),
            pl.BlockSpec((1, d), lambda i, j: (0, 0)),
            pl.BlockSpec((None, d, tf), lambda i, j: (layer, 0, j)),
            pl.BlockSpec((None, d, tf), lambda i, j: (layer, 0, j + nj)),
            pl.BlockSpec((None, tf, d), lambda i, j: (layer, j, 0)),
        ],
        out_specs=pl.BlockSpec((tm, d), lambda i, j: (i, 0)),
        out_shape=jax.ShapeDtypeStruct((m, d), F32),
        scratch_shapes=[pltpu.VMEM((tm, d), BF16), pltpu.VMEM((tm, d), F32)],
        compiler_params=_cparams("parallel", "arbitrary"),
        name="ffn_block",
    )(x, ga, gb, w_gu, w_gu, w_d)


N_BF_BLOCKS = 12
N_F32_BLOCKS = 4
_W_STEP_BLOCK = (0, 1, 2, 3, 4, 7, 10, 11, 12, 13, 14, 15, 5, 6, 8, 9)


def _w_block(j):
    blk = _W_STEP_BLOCK[-1]
    for step in range(len(_W_STEP_BLOCK) - 2, -1, -1):
        blk = jnp.where(j == step, _W_STEP_BLOCK[step], blk)
    return blk


def _inproj_kernel(x_ref, g_ref, w_ref, wgate_ref, fk_in, fv_in, bk_in, bv_in,
                   pbf_ref, fk_ref, fv_ref, bk_ref, bv_ref, gate_ref, *rest, with_gate_t):
    del fk_in, fv_in, bk_in, bv_in
    if with_gate_t:
        gate_t_ref, xn_ref = rest
    else:
        (xn_ref,) = rest
    j = pl.program_id(1)
    tm = x_ref.shape[0]

    @pl.when(j == 0)
    def _():
        xn = _rms(x_ref[...], g_ref[...]).astype(BF16)
        xn_ref[...] = xn
        gate = _dot(xn, wgate_ref[...])
        gate_ref[...] = gate
        if with_gate_t:
            gate_t_ref[...] = gate.T

    @pl.when(j < N_BF_BLOCKS)
    def _():
        pbf_ref[...] = _dot(xn_ref[...], w_ref[...]).astype(BF16)

    for k, ref in enumerate((fk_ref, fv_ref, bk_ref, bv_ref)):
        @pl.when(j == N_BF_BLOCKS + k)
        def _(ref=ref):
            y = _dot(xn_ref[...], w_ref[...])
            for h in range(N_HEADS):
                ref[pl.ds(h, tm, stride=N_HEADS), :] = y[:, h * HEAD_DIM:(h + 1) * HEAD_DIM]


def _inproj(x, g, w_all, w_gate, kv, *, layer, tm, with_gate_t):
    m, d = x.shape
    nb = N_BF_BLOCKS + N_F32_BLOCKS
    rows = pl.BlockSpec((None, N_HEADS * tm, HEAD_DIM), lambda i, j: (layer, i, 0))
    out_shape = [jax.ShapeDtypeStruct((m, N_BF_BLOCKS * W_MIX), BF16)]
    out_shape += [jax.ShapeDtypeStruct(a.shape, F32) for a in kv]
    out_shape += [jax.ShapeDtypeStruct((m, LANES), F32)]
    out_specs = [
        pl.BlockSpec((tm, W_MIX), lambda i, j: (i, jnp.minimum(j, N_BF_BLOCKS - 1))),
        rows, rows, rows, rows,
        pl.BlockSpec((tm, LANES), lambda i, j: (i, 0)),
    ]
    if with_gate_t:
        out_shape.append(jax.ShapeDtypeStruct((LANES, m), F32))
        out_specs.append(pl.BlockSpec((LANES, tm), lambda i, j: (0, i)))
    untouched = pl.BlockSpec(memory_space=pl.ANY)
    outs = pl.pallas_call(
        functools.partial(_inproj_kernel, with_gate_t=with_gate_t),
        grid=(m // tm, nb),
        in_specs=[
            pl.BlockSpec((tm, d), lambda i, j: (i, 0)),
            pl.BlockSpec((1, d), lambda i, j: (0, 0)),
            pl.BlockSpec((None, d, W_MIX), lambda i, j: (layer, 0, _w_block(j))),
            pl.BlockSpec((None, d, LANES), lambda i, j: (layer, 0, 0)),
            untouched, untouched, untouched, untouched,
        ],
        out_specs=out_specs,
        out_shape=out_shape,
        input_output_aliases={4: 1, 5: 2, 6: 3, 7: 4},
        scratch_shapes=[pltpu.VMEM((tm, d), BF16)],
        compiler_params=_cparams("parallel", "arbitrary"),
        name="inproj",
    )(x, g, w_all, w_gate, *kv)
    return (outs[0], tuple(outs[1:5])) + tuple(outs[5:])


def _merge_kernel(x_ref, mo_ref, fo_ref, bo_ref, gm_ref, gf_ref, gb_ref, g_ref,
                  wm_ref, wf_ref, wb_ref, wo_ref, o_ref):
    merged = (_sigmoid(gm_ref[...].astype(F32)) * _dot(mo_ref[...], wm_ref[...])
              + _sigmoid(gf_ref[...].astype(F32)) * _dot(fo_ref[...], wf_ref[...])
              + _sigmoid(gb_ref[...].astype(F32)) * _dot(bo_ref[...], wb_ref[...]))
    y = _dot(merged.astype(BF16), wo_ref[...])
    o_ref[...] = x_ref[...] + _rms(y, g_ref[...])


def _merge_out(x, m_out, f_out, b_out, pbf, g, w_m, w_f, w_b, w_o, *, layer, tm):
    m, d = x.shape
    gate_blk = d // W_MIX

    def gate_spec(k):
        return pl.BlockSpec((tm, d), lambda i: (i, (6 + k * gate_blk) // gate_blk))

    row = pl.BlockSpec((tm, W_MIX), lambda i: (i, 0))
    wbr = pl.BlockSpec((None, W_MIX, d), lambda i: (layer, 0, 0))
    return pl.pallas_call(
        _merge_kernel,
        grid=(m // tm,),
        in_specs=[
            pl.BlockSpec((tm, d), lambda i: (i, 0)), row, row, row,
            gate_spec(0), gate_spec(1), gate_spec(2),
            pl.BlockSpec((1, d), lambda i: (0, 0)),
            wbr, wbr, wbr, pl.BlockSpec((None, d, d), lambda i: (layer, 0, 0)),
        ],
        out_specs=pl.BlockSpec((tm, d), lambda i: (i, 0)),
        out_shape=jax.ShapeDtypeStruct((m, d), F32),
        compiler_params=_cparams("parallel"),
        name="merge_out",
    )(x, m_out, f_out, b_out, pbf, pbf, pbf, g, w_m, w_f, w_b, w_o)


def _w_in_offsets(d_model):
    widths = (("m_q", W_MIX), ("m_k", W_MIX), ("m_v", W_MIX), ("m_o", W_MIX), ("m_i", N_HEADS), ("m_f", N_HEADS),
              ("f_q", W_MIX), ("f_k", W_MIX), ("f_v", W_MIX), ("f_f", N_HEADS),
              ("b_q", W_MIX), ("b_k", W_MIX), ("b_v", W_MIX),
              ("g_m", d_model), ("g_f", d_model), ("g_b", d_model))
    out, off = {}, 0
    for name, w in widths:
        out[name] = (off, off + w)
        off += w
    return out


def _prep_w_in(w_in):
    d = w_in.shape[-2]
    offs = _w_in_offsets(d)

    def cols(name):
        a, b = offs[name]
        return w_in[..., a:b]

    w_all = jnp.concatenate([w_in[..., :offs["m_i"][0]], w_in[..., offs["m_f"][1]:offs["f_f"][0]],
                             w_in[..., offs["f_f"][1]:]], axis=-1).astype(BF16)
    gate = jnp.concatenate([cols("m_i"), cols("m_f"), cols("f_f")], axis=-1)
    pad = [(0, 0)] * (gate.ndim - 1) + [(0, LANES - gate.shape[-1])]
    w_gate = jnp.pad(gate, pad).astype(BF16)
    return w_all, w_gate


def _gateprep_kernel(gate_ref, gate_t_ref, bias_ref, bias_t_ref,
                     act_ref, cl_ref, act_t_ref, cl_t_ref, kb_ref, qb_ref, cf_ref, *, chunk):
    t = gate_ref.shape[0]
    lane = lax.broadcasted_iota(jnp.int32, (1, LANES), 1)
    z = gate_ref[...] + bias_ref[...]
    act_ref[...] = jnp.where(lane < N_HEADS, z, _log_sigmoid(z))
    sub = lax.broadcasted_iota(jnp.int32, (LANES, 1), 0)
    zt = gate_t_ref[...] + bias_t_ref[...]
    act_t_ref[...] = jnp.where(sub < N_HEADS, zt, _log_sigmoid(zt))

    r = lax.broadcasted_iota(jnp.int32, (chunk, chunk), 0)
    c = lax.broadcasted_iota(jnp.int32, (chunk, chunk), 1)
    tril = jnp.where(c <= r, 1.0, 0.0).astype(BF16)
    triu = jnp.where(r <= c, 1.0, 0.0).astype(BF16)
    carry = jnp.zeros((1, LANES), F32)
    for ci in range(t // chunk):
        sl = slice(ci * chunk, (ci + 1) * chunk)
        loc = _dot3_r(tril, act_ref[sl, :])
        cl_ref[sl, :] = loc
        full = loc + carry
        cf_ref[sl, :] = full
        carry = full[chunk - 1:chunk, :]
        cl_t_ref[:, sl] = _dot3_l(act_t_ref[:, sl], triu)

    terms = _split3(cf_ref[...] * (HEAD_DIM ** 0.5))
    r = lax.broadcasted_iota(jnp.int32, (LANES, W_MIX), 0)
    c = lax.broadcasted_iota(jnp.int32, (LANES, W_MIX), 1)
    own = r == 2 * N_HEADS + c // HEAD_DIM
    lane = lax.broadcasted_iota(jnp.int32, (1, W_MIX), 1) % HEAD_DIM

    def place(first):
        return sum(_dot(x, jnp.where(own & (c % HEAD_DIM == first + j), 1.0, 0.0).astype(BF16))
                   for j, x in enumerate(terms))

    kb_ref[...] = jnp.where((lane >= 3) & (lane < 6), 1.0, -place(0)).astype(BF16)
    qb_ref[...] = jnp.where(lane < 3, 1.0, place(3)).astype(BF16)


def _gateprep(gate, gate_t, bias, bias_t, *, n_seq, chunk):
    t = gate.shape[0] // n_seq
    col = pl.BlockSpec((t, LANES), lambda b: (b, 0))
    row = pl.BlockSpec((LANES, t), lambda b: (0, b))
    col_s = jax.ShapeDtypeStruct(gate.shape, F32)
    row_s = jax.ShapeDtypeStruct(gate_t.shape, F32)
    dec = pl.BlockSpec((t, W_MIX), lambda b: (b, 0))
    dec_s = jax.ShapeDtypeStruct((gate.shape[0], W_MIX), BF16)
    return pl.pallas_call(
        functools.partial(_gateprep_kernel, chunk=chunk),
        grid=(n_seq,),
        in_specs=[col, row, pl.BlockSpec((1, LANES), lambda b: (0, 0)), pl.BlockSpec((LANES, 1), lambda b: (0, 0))],
        out_specs=[col, col, row, row, dec, dec],
        out_shape=[col_s, col_s, row_s, row_s, dec_s, dec_s],
        scratch_shapes=[pltpu.VMEM((t, LANES), F32)],
        compiler_params=_cparams("parallel"),
        name="gateprep",
    )(gate, gate_t, bias, bias_t)


def _mlstm_kernel(q_ref, k_ref, v_ref, o_ref, wq_ref, wk_ref, hn_ref, act_ref, cl_ref, act_t_ref, cl_t_ref,
                  out_ref, caug_ref, m_ref, cbuf, qs, kts, icb, bcb, cst, *, chunk):
    h = pl.program_id(1)
    t = q_ref.shape[0]
    reps = chunk // LANES

    def wide(a, n):
        return jnp.concatenate([a] * n, axis=1) if n > 1 else a

    def conv_silu(x_ref, w_ref):
        cbuf[0:8, :] = jnp.zeros((8, LANES), F32)
        cbuf[8:, :] = x_ref[...].astype(F32)
        y = w_ref[0:1, :] * cbuf[pl.ds(8 - (CONV_W - 1), t), :]
        for j in range(1, CONV_W):
            y = y + w_ref[j:j + 1, :] * cbuf[pl.ds(8 - (CONV_W - 1) + j, t), :]
        return y * _sigmoid(y)

    qs[...] = conv_silu(q_ref, wq_ref).astype(BF16)
    kts[...] = (conv_silu(k_ref, wk_ref) * (HEAD_DIM ** -0.5)).T.astype(BF16)

    rr = lax.broadcasted_iota(jnp.int32, (LANES, LANES), 0)
    icb[...] = _dot3_l(act_ref[...], jnp.where(rr == h, 1.0, 0.0).astype(BF16))
    bcb[...] = _dot3_l(cl_ref[...], jnp.where(rr == N_HEADS + h, 1.0, 0.0).astype(BF16))

    cst[...] = jnp.zeros_like(cst)
    tt = lax.broadcasted_iota(jnp.int32, (chunk, chunk), 0)
    ss = lax.broadcasted_iota(jnp.int32, (chunk, chunk), 1)
    ones = jnp.ones((chunk, LANES), BF16)

    def body(c, m_prev):
        r0 = pl.multiple_of(c * chunk, chunk)
        rows = pl.ds(r0, chunk)
        qc = qs[rows, :]
        ktc = kts[:, rows]
        v_aug = jnp.concatenate([v_ref[rows, :], ones], axis=1)
        i_r = act_t_ref[pl.ds(h, 1), rows]
        b_r = cl_t_ref[pl.ds(N_HEADS + h, 1), rows]
        i_c = icb[rows, :]
        b_c = bcb[rows, :]
        dmat = jnp.where(ss <= tt, wide(b_c, reps) + (i_r - b_r), -jnp.inf)
        inter = b_c + m_prev
        m_t = jnp.maximum(inter, jnp.max(dmat, axis=-1, keepdims=True))
        w_inter = jnp.exp(inter - m_t)
        s = (_dot(qc, ktc) * jnp.exp(dmat - wide(m_t, reps))).astype(BF16)
        c_prev = cst[...]
        num = wide(w_inter, 2) * _dot(qc, c_prev.astype(BF16)) + _dot(s, v_aug)
        hh = num[:, :LANES] / jnp.maximum(jnp.abs(num[:, LANES:]), jnp.exp(-m_t))
        hh = hh * lax.rsqrt(jnp.mean(hh * hh, axis=-1, keepdims=True) + EPS) * hn_ref[...]
        out_ref[rows, :] = (_sigmoid(o_ref[rows, :].astype(F32)) * hh).astype(BF16)

        b_last = b_r[:, chunk - 1:chunk]
        m_new = jnp.maximum(b_last + m_prev, jnp.max(b_last - b_r + i_r, axis=-1, keepdims=True))
        a_prev = jnp.exp(b_last + m_prev - m_new)
        wg_c = jnp.exp(b_last - b_c + i_c - m_new)
        wv = (wide(wg_c, 2) * v_aug.astype(F32)).astype(BF16)
        cst[...] = a_prev * c_prev + _dot(ktc, wv)
        return m_new

    m_fin = lax.fori_loop(0, t // chunk, body, jnp.full((1, 1), -jnp.inf, F32),
                          unroll=min(MLSTM_UNROLL, t // chunk))
    caug_ref[...] = cst[...]
    m_ref[...] = jnp.broadcast_to(m_fin, m_ref.shape)


def _mlstm_prompt(pbf, w_conv, g_hn, act, cl, act_t, cl_t, *, n_seq, chunk):
    bt = pbf.shape[0]
    t = bt // n_seq

    def head_cols(blk):
        return pl.BlockSpec((t, HEAD_DIM), lambda b, h: (b, blk * N_HEADS + h))

    col = pl.BlockSpec((t, LANES), lambda b, h: (b, 0))
    row = pl.BlockSpec((LANES, t), lambda b, h: (0, b))
    return pl.pallas_call(
        functools.partial(_mlstm_kernel, chunk=chunk),
        grid=(n_seq, N_HEADS),
        in_specs=[
            head_cols(0), head_cols(1), head_cols(2), head_cols(3),
            pl.BlockSpec((CONV_W, HEAD_DIM), lambda b, h: (0, h)),
            pl.BlockSpec((CONV_W, HEAD_DIM), lambda b, h: (0, N_HEADS + h)),
            pl.BlockSpec((1, HEAD_DIM), lambda b, h: (0, h)),
            col, col, row, row,
        ],
        out_specs=[
            pl.BlockSpec((t, HEAD_DIM), lambda b, h: (b, h)),
            pl.BlockSpec((None, None, HEAD_DIM, 2 * LANES), lambda b, h: (b, h, 0, 0)),
            pl.BlockSpec((None, None, 8, LANES), lambda b, h: (b, h, 0, 0)),
        ],
        out_shape=[
            jax.ShapeDtypeStruct((bt, W_MIX), BF16),
            jax.ShapeDtypeStruct((n_seq, N_HEADS, HEAD_DIM, 2 * LANES), F32),
            jax.ShapeDtypeStruct((n_seq, N_HEADS, 8, LANES), F32),
        ],
        scratch_shapes=[
            pltpu.VMEM((t + 8, LANES), F32), pltpu.VMEM((t, HEAD_DIM), BF16), pltpu.VMEM((HEAD_DIM, t), BF16),
            pltpu.VMEM((t, LANES), F32), pltpu.VMEM((t, LANES), F32), pltpu.VMEM((HEAD_DIM, 2 * LANES), F32),
        ],
        compiler_params=_cparams("parallel", "arbitrary"),
        name="mlstm_prompt",
    )(pbf, pbf, pbf, pbf, w_conv, w_conv, g_hn, act, cl, act_t, cl_t)


def _attn_kernel(*refs, moba, n_slots):
    n_in = 3 if moba else 5
    q_ref, k_ref, v_ref = refs[:3]
    o_ref = refs[n_in]
    kaug, vt = refs[n_in + 1:n_in + 3]
    per_slot = refs[n_in + 3:n_in + 3 + 2 * n_slots]
    if moba:
        kmean = refs[n_in + 3 + 2 * n_slots]
        sels = refs[n_in + 4 + 2 * n_slots:]
    else:
        kb_ref, qb_ref = refs[3:5]
        sels = (None,) * n_slots
    slots = [(per_slot[2 * i], per_slot[2 * i + 1], sels[i]) for i in range(n_slots)]
    jp = pl.program_id(2)
    t = kaug.shape[0]
    blk = slots[0][0].shape[0]
    n_blk = t // blk
    scale = HEAD_DIM ** -0.5
    c_exp = scale * math.log2(math.e)

    @pl.when(jp == 0)
    def _():
        own_rows = pl.ds(pl.program_id(1), t, stride=N_HEADS)
        k = k_ref[own_rows, :]
        kaug[:, :HEAD_DIM] = k.astype(BF16)
        vt[...] = v_ref[own_rows, :].T.astype(BF16)
        if moba:
            lane = lax.broadcasted_iota(jnp.int32, (t, LANES), 1)
            kblk = lax.broadcasted_iota(jnp.int32, (t, LANES), 0) // blk
            kaug[:, HEAD_DIM:] = jnp.where(lane == kblk, 1.0, 0.0).astype(BF16)
            kmean[...] = jnp.zeros_like(kmean)
            for n in range(n_blk):
                kmean[n:n + 1, :] = jnp.mean(k[n * blk:(n + 1) * blk, :], axis=0, keepdims=True)
        else:
            kaug[:, HEAD_DIM:] = kb_ref[...]

    ks = lax.broadcasted_iota(jnp.int32, (blk, blk), 0)
    qs = lax.broadcasted_iota(jnp.int32, (blk, blk), 1)

    def attend(qi, slot):
        qaug, pbuf, selb = slots[slot]
        rows = slice(qi * blk, (qi + 1) * blk)
        qb = q_ref[rows, :]
        qaug[:, :HEAD_DIM] = qb
        if moba:
            nbp = kmean.shape[0]
            row = lax.broadcasted_iota(jnp.int32, (nbp, blk), 0)
            km_hi, km_mid, km_lo = _split3(kmean[...])
            gs = _dot_nt(km_hi, qb) + _dot_nt(km_mid, qb) + _dot_nt(km_lo, qb)
            gm = jnp.where(row < qi, gs, -jnp.inf)
            selb[...] = jnp.zeros_like(selb)
            for n in range(qi):
                g_n = gm[n:n + 1, :]
                beats = (gm > g_n) | ((gm == g_n) & (row < n))
                cnt = jnp.sum(jnp.where(beats, 1.0, 0.0), axis=0, keepdims=True)
                selb[n:n + 1, :] = jnp.where(cnt < MOBA_TOPK, 0.0, NEG / scale)
            qaug[:, HEAD_DIM:] = selb[...].T.astype(BF16)
        else:
            qaug[:, HEAD_DIM:] = qb_ref[rows, :]
        qa = qaug[...]

        def raw(c):
            s = _dot_nt(kaug[c * blk:(c + 1) * blk, :], qa)
            return jnp.where(ks <= qs, s, NEG) if c == qi else s

        m = jnp.max(raw(qi), axis=0, keepdims=True)
        for c in range(qi):
            m = jnp.maximum(m, jnp.max(raw(c), axis=0, keepdims=True))
        l = jnp.zeros((1, blk), F32)
        for c in range(qi + 1):
            p = jnp.exp2((raw(c) - m) * c_exp)
            l = l + jnp.sum(p, axis=0, keepdims=True)
            pbuf[c * blk:(c + 1) * blk, :] = p.astype(BF16)
        nk = (qi + 1) * blk
        acc = _dot(vt[:, 0:nk], pbuf[0:nk, :])
        o_ref[rows, :] = (acc / l).T.astype(BF16)

    n_pairs = -(-n_blk // 2)
    per_step = n_slots // 2

    def step(s):
        for r in range(per_step):
            j = s * per_step + r
            if j < n_pairs:
                attend(j, 2 * r)
                if n_blk - 1 - j != j:
                    attend(n_blk - 1 - j, 2 * r + 1)

    for s in range(-(-n_pairs // per_step)):
        pl.when(jp == s)(functools.partial(step, s))


def _attn_prompt(pbf, k, v, kbias, qbias, *, layer, n_seq, q_block, moba):
    bt = pbf.shape[0]
    t = bt // n_seq
    blk = MOBA_BLOCK
    nq = t // blk
    assert nq <= LANES
    head = pl.BlockSpec((t, HEAD_DIM), lambda b, h, j: (b, h))
    kv = pl.BlockSpec((None, N_HEADS * t, HEAD_DIM), lambda b, h, j: (layer, b, 0))
    in_specs = [pl.BlockSpec((t, HEAD_DIM), lambda b, h, j: (b, q_block * N_HEADS + h)), kv, kv]
    args = [pbf, k, v]
    n_slots = 2 * ATTN_PAIRS_PER_STEP
    slot = [pltpu.VMEM((blk, 2 * HEAD_DIM), BF16), pltpu.VMEM((t, blk), BF16)]
    scratch = [pltpu.VMEM((t, 2 * HEAD_DIM), BF16), pltpu.VMEM((HEAD_DIM, t), BF16)] + slot * n_slots
    if moba:
        nbp = -(-nq // 8) * 8
        scratch += [pltpu.VMEM((nbp, HEAD_DIM), F32)] + [pltpu.VMEM((LANES, blk), F32)] * n_slots
    else:
        in_specs += [head, head]
        args += [kbias, qbias]
    return pl.pallas_call(
        functools.partial(_attn_kernel, moba=moba, n_slots=n_slots),
        grid=(n_seq, N_HEADS, -(-(-(-nq // 2)) // ATTN_PAIRS_PER_STEP)),
        in_specs=in_specs,
        out_specs=head,
        out_shape=jax.ShapeDtypeStruct((bt, W_MIX), BF16),
        scratch_shapes=scratch,
        compiler_params=_cparams("parallel", "arbitrary", "arbitrary"),
        name="moba_prompt" if moba else "fox_prompt",
    )(*args)


def _mlstm_step_kernel(p_ref, gate_ref, bias_ref, conv_ref, wc_ref, hn_ref, c0_ref, n0_ref, m0_ref,
                       out_ref, c_ref, n_ref, m_ref):
    x = p_ref[...].astype(F32)
    qk = wc_ref[CONV_W - 1:CONV_W, :] * x[:, :2 * W_MIX]
    for j in range(CONV_W - 1):
        qk = qk + wc_ref[j:j + 1, :] * conv_ref[j:j + 1, :]
    qk = qk * _sigmoid(qk)
    q = qk[:, :W_MIX]
    k = qk[:, W_MIX:] * (HEAD_DIM ** -0.5)
    v = x[:, 2 * W_MIX:3 * W_MIX]
    og = _sigmoid(x[:, 3 * W_MIX:])
    z = gate_ref[...] + bias_ref[...]
    logf = _log_sigmoid(z)
    lane = lax.broadcasted_iota(jnp.int32, (1, LANES), 1)
    m_row = jnp.zeros((1, LANES), F32)
    outs = []
    for h in range(N_HEADS):
        hs = slice(h * HEAD_DIM, (h + 1) * HEAD_DIM)
        i_h = z[:, h:h + 1]
        f_h = logf[:, N_HEADS + h:N_HEADS + h + 1]
        m0 = m0_ref[:, h:h + 1]
        m_new = jnp.maximum(f_h + m0, i_h)
        a = jnp.exp(f_h + m0 - m_new)
        wgt = jnp.exp(i_h - m_new)
        k_col = jnp.broadcast_to(k[:, hs], (HEAD_DIM, HEAD_DIM)).T
        q_col = jnp.broadcast_to(q[:, hs], (HEAD_DIM, HEAD_DIM)).T
        c_new = a * c0_ref[h] + (wgt * k_col) * v[:, hs]
        n_new = a * n0_ref[h:h + 1, :] + wgt * k[:, hs]
        c_ref[h] = c_new
        n_ref[h:h + 1, :] = n_new
        num = jnp.sum(q_col * c_new, axis=0, keepdims=True)
        den = jnp.sum(q[:, hs] * n_new, axis=-1, keepdims=True)
        hh = num / jnp.maximum(jnp.abs(den), jnp.exp(-m_new))
        hh = hh * lax.rsqrt(jnp.mean(hh * hh, axis=-1, keepdims=True) + EPS) * hn_ref[:, hs]
        outs.append(og[:, hs] * hh)
        m_row = jnp.where(lane == h, m_new, m_row)
    out_ref[...] = jnp.concatenate(outs, axis=1).astype(BF16)
    m_ref[...] = m_row


def _mlstm_step(pbf3, gate3, bias, conv_prev, w_conv, g_hn, c0, n0, m0):
    bs = pbf3.shape[0]
    return pl.pallas_call(
        _mlstm_step_kernel,
        grid=(bs,),
        in_specs=[
            pl.BlockSpec((None, 1, 4 * W_MIX), lambda b: (b, 0, 0)),
            pl.BlockSpec((None, 1, LANES), lambda b: (b, 0, 0)),
            pl.BlockSpec((1, LANES), lambda b: (0, 0)),
            pl.BlockSpec((None, CONV_W - 1, 2 * W_MIX), lambda b: (b, 0, 0)),
            pl.BlockSpec((CONV_W, 2 * W_MIX), lambda b: (0, 0)),
            pl.BlockSpec((1, W_MIX), lambda b: (0, 0)),
            pl.BlockSpec((None, N_HEADS, HEAD_DIM, HEAD_DIM), lambda b: (b, 0, 0, 0)),
            pl.BlockSpec((None, N_HEADS, HEAD_DIM), lambda b: (b, 0, 0)),
            pl.BlockSpec((None, 1, N_HEADS), lambda b: (b, 0, 0)),
        ],
        out_specs=[
            pl.BlockSpec((None, 1, W_MIX), lambda b: (b, 0, 0)),
            pl.BlockSpec((None, N_HEADS, HEAD_DIM, HEAD_DIM), lambda b: (b, 0, 0, 0)),
            pl.BlockSpec((None, N_HEADS, HEAD_DIM), lambda b: (b, 0, 0)),
            pl.BlockSpec((None, 1, LANES), lambda b: (b, 0, 0)),
        ],
        out_shape=[
            jax.ShapeDtypeStruct((bs, 1, W_MIX), BF16),
            jax.ShapeDtypeStruct(c0.shape, F32),
            jax.ShapeDtypeStruct(n0.shape, F32),
            jax.ShapeDtypeStruct((bs, 1, LANES), F32),
        ],
        compiler_params=_cparams("parallel"),
        name="mlstm_step",
    )(pbf3, gate3, bias, conv_prev, w_conv, g_hn, c0, n0, m0)


PAGE_ROWS = PAGE * N_HEADS


def _pad_rows8(a):
    return jnp.concatenate([a, jnp.zeros_like(a)], axis=0)


def _decode_kernel(pt_ref, *refs, group, n_pages):
    del pt_ref
    pages = refs[:4 * group]
    (fq_ref, bq_ref, knew_ref, vnew_ref, gate_ref, bias_ref,
     out_ref, lf_ref, top_ref, qf, qb, m_s, l_s, acc, carry, ksum, msuf, mtot) = refs[4 * group:]
    first = (pl.program_id(0) == 0) & (pl.program_id(1) == 0)
    g = pl.program_id(1)
    n_blocks = n_pages // 2
    scale = HEAD_DIM ** -0.5

    @pl.when(first)
    def _():
        src = lax.broadcasted_iota(jnp.int32, (PAGE_ROWS, PAGE_ROWS), 0)
        dst = lax.broadcasted_iota(jnp.int32, (PAGE_ROWS, PAGE_ROWS), 1)
        same_head = (src & (N_HEADS - 1)) == (dst & (N_HEADS - 1))
        mtot[...] = jnp.where(same_head, 1.0, 0.0).astype(BF16)
        msuf[...] = jnp.where(same_head & (src > dst), 1.0, 0.0).astype(BF16)

    @pl.when(g == 0)
    def _():
        qf[...] = _pad_rows8(fq_ref[...].astype(F32))
        qb[...] = _pad_rows8(bq_ref[...].astype(F32))
        m_s[...] = jnp.full(m_s.shape, NEG, F32)
        l_s[...] = jnp.zeros_like(l_s)
        acc[...] = jnp.zeros_like(acc)
        lf_new = _log_sigmoid(gate_ref[...] + bias_ref[...])
        lf_ref[...] = lf_new
        r = lax.broadcasted_iota(jnp.int32, (LANES, PAGE_ROWS), 0)
        c = lax.broadcasted_iota(jnp.int32, (LANES, PAGE_ROWS), 1)
        spread = jnp.where(r == 2 * N_HEADS + (c & (N_HEADS - 1)), 1.0, 0.0).astype(BF16)
        carry[...] = _dot3_l(jnp.broadcast_to(lf_new, (8, LANES)), spread)[0:1]
        ksum[...] = jnp.zeros_like(ksum)

    row = lax.broadcasted_iota(jnp.int32, (8, PAGE_ROWS), 0)
    lane = lax.broadcasted_iota(jnp.int32, (8, PAGE_ROWS), 1)
    own_head = (lane & (N_HEADS - 1)) == row
    wide = lambda a: jnp.concatenate([a] * N_HEADS, axis=1)

    lf_rows = jnp.concatenate([pages[4 * i + 3][...] for i in range(group)], axis=0)
    suf = _dot3_l(lf_rows, msuf[...])
    tot = _dot3_l(lf_rows, mtot[...])
    run = carry[...]
    q_f = qf[...].astype(BF16)
    s_all = []
    for i in range(group):
        s = _dot_nt(q_f, pages[4 * i][...].astype(BF16)) * scale + (suf[i:i + 1] + run)
        s_all.append(jnp.where(own_head, s, NEG))
        run = run + tot[i:i + 1]
    carry[...] = run
    m_old = m_s[...]
    m_new = m_old
    for s in s_all:
        m_new = jnp.maximum(m_new, jnp.max(s, axis=-1, keepdims=True))
    alpha = jnp.exp(m_old - m_new)
    l_new = alpha * l_s[...]
    a_new = alpha * acc[...]
    m_wide = wide(m_new)
    for i, s in enumerate(s_all):
        p = jnp.exp(s - m_wide)
        l_new = l_new + jnp.sum(p, axis=-1, keepdims=True)
        a_new = a_new + _dot(p.astype(BF16), pages[4 * i + 1][...].astype(BF16))
    m_s[...] = m_new
    l_s[...] = l_new
    acc[...] = a_new

    sub8 = lax.broadcasted_iota(jnp.int32, (8, HEAD_DIM), 0)
    for i in range(group):
        vs = jnp.sum(pages[4 * i + 2][...].reshape(PAGE_ROWS // 8, 8, HEAD_DIM), axis=0)
        per_head = jnp.where(sub8 < N_HEADS, vs + pltpu.roll(vs, N_HEADS, 0), 0.0)
        r0 = pl.multiple_of((n_blocks - 1 - (g * group + i) // 2) * 8, 8)
        ksum[pl.ds(r0, 8), :] = ksum[pl.ds(r0, 8), :] + per_head

    @pl.when(g == pl.num_programs(1) - 1)
    def _():
        s_new = jnp.sum(qf[...] * _pad_rows8(knew_ref[...]), axis=-1, keepdims=True) * scale
        m_old = m_s[...]
        m_fin = jnp.maximum(m_old, s_new)
        alpha = jnp.exp(m_old - m_fin)
        p_new = jnp.exp(s_new - m_fin)
        l_fin = alpha * l_s[...] + p_new
        o = (alpha * acc[...] + p_new * _pad_rows8(vnew_ref[...])) / l_fin
        out_ref[...] = o[0:N_HEADS].astype(BF16)
        km_hi, km_mid, km_lo = _split3(ksum[...] * (1.0 / MOBA_BLOCK))
        q_b = qb[...].astype(BF16)
        gs = _dot_nt(q_b, km_hi) + _dot_nt(q_b, km_mid) + _dot_nt(q_b, km_lo)
        grow = lax.broadcasted_iota(jnp.int32, gs.shape, 0)
        glane = lax.broadcasted_iota(jnp.int32, gs.shape, 1)
        gs = jnp.where((glane & 7) == grow, gs, -jnp.inf)
        glane_f = glane.astype(F32)
        tlane = lax.broadcasted_iota(jnp.int32, (8, LANES), 1)
        top = jnp.zeros((8, LANES), F32)
        for slot in range(MOBA_TOPK):
            best = jnp.max(gs, axis=-1, keepdims=True)
            idx = jnp.min(jnp.where(gs == best, glane_f, float(8 * n_blocks)), axis=-1, keepdims=True)
            top = jnp.where(tlane == slot, idx, top)
            gs = jnp.where(glane_f == idx, -jnp.inf, gs)
        top_ref[...] = jnp.right_shift(top.astype(jnp.int32), 3)


def _decode_attn(page_table, cfk, cfv, cbk, lfc, fq4, bq4, fk4, fv4, gate3, bias, *, layer, group):
    bs, n_pages = page_table.shape
    assert n_pages % group == 0 and group % 2 == 0 and n_pages // 2 >= MOBA_TOPK

    def page_spec(i, shape):
        def imap(b, g, pt):
            return (layer, pt[b * n_pages + (n_pages - 1 - (g * group + i))], 0, 0)
        return pl.BlockSpec((None, None) + shape, imap)

    in_specs, args = [], []
    for i in range(group):
        in_specs += [page_spec(i, (PAGE_ROWS, HEAD_DIM)), page_spec(i, (PAGE_ROWS, HEAD_DIM)),
                     page_spec(i, (PAGE_ROWS, HEAD_DIM)), page_spec(i, (1, PAGE_ROWS))]
        args += [cfk, cfv, cbk, lfc]
    per_head = pl.BlockSpec((None, N_HEADS, HEAD_DIM), lambda b, g, pt: (b, 0, 0))
    in_specs += [per_head, per_head, per_head, per_head,
                 pl.BlockSpec((None, 1, LANES), lambda b, g, pt: (b, 0, 0)),
                 pl.BlockSpec((1, LANES), lambda b, g, pt: (0, 0))]
    args += [fq4, bq4, fk4, fv4, gate3, bias]
    small = pltpu.VMEM((8, LANES), F32)
    return pl.pallas_call(
        functools.partial(_decode_kernel, group=group, n_pages=n_pages),
        grid_spec=pltpu.PrefetchScalarGridSpec(
            num_scalar_prefetch=1,
            grid=(bs, n_pages // group),
            in_specs=in_specs,
            out_specs=[
                per_head,
                pl.BlockSpec((None, 1, LANES), lambda b, g, pt: (b, 0, 0)),
                pl.BlockSpec((None, 8, LANES), lambda b, g, pt: (b, 0, 0)),
            ],
            scratch_shapes=[
                small, small, small, small, small,
                pltpu.VMEM((1, PAGE_ROWS), F32),
                pltpu.VMEM((8 * (n_pages // 2), HEAD_DIM), F32),
                pltpu.VMEM((PAGE_ROWS, PAGE_ROWS), BF16), pltpu.VMEM((PAGE_ROWS, PAGE_ROWS), BF16),
            ],
        ),
        out_shape=[
            jax.ShapeDtypeStruct((bs, N_HEADS, HEAD_DIM), BF16),
            jax.ShapeDtypeStruct((bs, 1, LANES), F32),
            jax.ShapeDtypeStruct((bs, 8, LANES), jnp.int32),
        ],
        compiler_params=_cparams("arbitrary", "arbitrary"),
        name="decode_attn",
    )(page_table.reshape(-1), *args)


MOBA_PAGES = 2 * MOBA_TOPK


def _moba_decode_kernel(pg_ref, k_hbm, v_hbm, q_ref, knew_ref, vnew_ref, out_ref, kbuf, vbuf, sem, *, layer):
    b = pl.program_id(0)
    slot = b % 2
    scale = HEAD_DIM ** -0.5

    def copies(seq, to_slot):
        out = []
        for h in range(N_HEADS):
            for j in range(MOBA_PAGES):
                i = h * MOBA_PAGES + j
                page = pg_ref[seq * N_HEADS * MOBA_PAGES + i]
                out.append(pltpu.make_async_copy(k_hbm.at[layer, page, :, h, :], kbuf.at[to_slot, i], sem.at[to_slot]))
                out.append(pltpu.make_async_copy(v_hbm.at[layer, page, :, h, :], vbuf.at[to_slot, i], sem.at[to_slot]))
        return out

    @pl.when(b == 0)
    def _():
        for c in copies(0, 0):
            c.start()

    @pl.when(b + 1 < pl.num_programs(0))
    def _():
        for c in copies(b + 1, 1 - slot):
            c.start()

    for c in copies(b, slot):
        c.wait()

    q = q_ref[...]
    rows = []
    for h in range(N_HEADS):
        q_h = q[h:h + 1, :]
        q8 = jnp.broadcast_to(q_h, (8, HEAD_DIM)).astype(BF16)
        pages = [h * MOBA_PAGES + i for i in range(MOBA_PAGES)]
        s = [_dot_nt(q8, kbuf[slot, i].astype(BF16))[0:1] * scale for i in pages]
        s_new = jnp.sum(q_h * knew_ref[h:h + 1, :], axis=-1, keepdims=True) * scale
        m = s_new
        for si in s:
            m = jnp.maximum(m, jnp.max(si, axis=-1, keepdims=True))
        p_new = jnp.exp(s_new - m)
        l = p_new
        o = p_new * vnew_ref[h:h + 1, :]
        for si, i in zip(s, pages):
            p = jnp.exp(si - m)
            l = l + jnp.sum(p, axis=-1, keepdims=True)
            o = o + _dot(jnp.broadcast_to(p, (8, PAGE)).astype(BF16), vbuf[slot, i].astype(BF16))[0:1]
        rows.append(o / l)
    out_ref[...] = jnp.concatenate(rows, axis=0).astype(BF16)


def _moba_decode(pages, cache_k, cache_v, bq, bk, bv, *, layer):
    bs = pages.shape[0]
    per_seq = pl.BlockSpec((None, N_HEADS, HEAD_DIM), lambda b, pg: (b, 0, 0))
    hbm = pl.BlockSpec(memory_space=pl.ANY)
    buf = pltpu.VMEM((2, N_HEADS * MOBA_PAGES, PAGE, HEAD_DIM), F32)
    return pl.pallas_call(
        functools.partial(_moba_decode_kernel, layer=layer),
        grid_spec=pltpu.PrefetchScalarGridSpec(
            num_scalar_prefetch=1,
            grid=(bs,),
            in_specs=[hbm, hbm, per_seq, per_seq, per_seq],
            out_specs=per_seq,
            scratch_shapes=[buf, buf, pltpu.SemaphoreType.DMA((2,))],
        ),
        out_shape=jax.ShapeDtypeStruct((bs, N_HEADS, HEAD_DIM), BF16),
        compiler_params=_cparams("arbitrary"),
        name="moba_decode",
    )(pages.reshape(-1), cache_k, cache_v, bq, bk, bv)


ROW_TILE = 1024
MERGE_TILE = 512
FF_TILE = 256
MLSTM_CHUNK = 256
DECODE_GROUP = 16
ATTN_PAIRS_PER_STEP = 4
MLSTM_UNROLL = 8


def kernel(x_prompt, x_sample, cache_fox_k, cache_fox_v, cache_fox_logf, cache_moba_k, cache_moba_v,
           state_mlstm_C, state_mlstm_n, state_mlstm_m, state_mlstm_conv, page_table,
           norm_g, w_ffn1_gu, w_ffn1_d, w_ffn2_gu, w_ffn2_d, w_in, b_mlstm_i, b_mlstm_f, b_fox_f,
           w_conv, g_headnorm, w_br_m, w_br_f, w_br_b, w_out):
    n_seq, t, d = x_prompt.shape
    bs, dec_t, _ = x_sample.shape
    depth, n_pool = cache_fox_k.shape[:2]
    assert dec_t == 1 and t % MOBA_BLOCK == 0 and cache_fox_k.shape[2:] == (PAGE, N_HEADS, HEAD_DIM)
    assert page_table.shape[1] % DECODE_GROUP == 0
    chunk = math.gcd(t, MLSTM_CHUNK)
    tm = math.gcd(n_seq * t, ROW_TILE)
    tm_merge = math.gcd(n_seq * t, MERGE_TILE)
    tf = math.gcd(w_ffn1_d.shape[1], FF_TILE)

    w_all, w_gate = _prep_w_in(w_in)
    wgu1, wd1, wgu2, wd2 = (w.astype(BF16) for w in (w_ffn1_gu, w_ffn1_d, w_ffn2_gu, w_ffn2_d))
    wbm, wbf, wbb, wo = (w.astype(BF16) for w in (w_br_m, w_br_f, w_br_b, w_out))
    bias = jnp.pad(jnp.concatenate([b_mlstm_i, b_mlstm_f, b_fox_f], axis=-1),
                   ((0, 0), (0, LANES - 3 * N_HEADS)))[:, None, :]
    bias_t = jnp.swapaxes(bias, 1, 2)
    paged = lambda c: c.reshape(depth, n_pool, PAGE_ROWS, HEAD_DIM)
    cfk, cfv, cbk = paged(cache_fox_k), paged(cache_fox_v), paged(cache_moba_k)
    lfc = cache_fox_logf.reshape(depth, n_pool, 1, PAGE_ROWS)
    hn = g_headnorm[:, None, :]

    xp = x_prompt.reshape(n_seq * t, d)
    xs = x_sample.reshape(bs, d)
    new_p, new_s = [], []
    kv_p = tuple(jnp.zeros((depth, N_HEADS * n_seq * t, HEAD_DIM), F32) for _ in range(4))
    kv_s = tuple(jnp.zeros((depth, N_HEADS * bs, HEAD_DIM), F32) for _ in range(4))
    for l in range(depth):
        g = [norm_g[l, i][None, :] for i in range(6)]
        xp = _ffn_block(xp, g[0], g[1], wgu1, wd1, layer=l, tm=tm, tf=tf)
        pbf, kv_p, gate, gate_t = _inproj(xp, g[2], w_all, w_gate, kv_p, layer=l, tm=tm, with_gate_t=True)
        act, cl, act_t, cl_t, kbias, qbias = _gateprep(gate, gate_t, bias[l], bias_t[l], n_seq=n_seq, chunk=chunk)
        m_out, caug, m_fin = _mlstm_prompt(pbf, w_conv[l], hn[l], act, cl, act_t, cl_t, n_seq=n_seq, chunk=chunk)
        f_out = _attn_prompt(pbf, kv_p[0], kv_p[1], kbias, qbias, layer=l, n_seq=n_seq, q_block=4, moba=False)
        b_out = _attn_prompt(pbf, kv_p[2], kv_p[3], None, None, layer=l, n_seq=n_seq, q_block=5, moba=True)
        xp = _merge_out(xp, m_out, f_out, b_out, pbf, g[3], wbm, wbf, wbb, wo, layer=l, tm=tm_merge)
        xp = _ffn_block(xp, g[4], g[5], wgu2, wd2, layer=l, tm=tm, tf=tf)
        conv_p = pbf.reshape(n_seq, t, -1)[:, t - (CONV_W - 1):, :2 * W_MIX].astype(F32)
        new_p.append((act[:, 2 * N_HEADS:3 * N_HEADS].reshape(n_seq, t, N_HEADS),
                      caug[..., :HEAD_DIM], caug[..., HEAD_DIM], m_fin[:, :, 0, 0], conv_p))
        xs = _ffn_block(xs, g[0], g[1], wgu1, wd1, layer=l, tm=bs, tf=tf)
        pbf_s, kv_s, gate_s = _inproj(xs, g[2], w_all, w_gate, kv_s, layer=l, tm=bs, with_gate_t=False)
        row3 = lambda a: a.reshape(bs, 1, -1)
        pbf3, gate3 = row3(pbf_s), row3(gate_s)
        m_out_s, c_s, n_s, m_s = _mlstm_step(pbf3, gate3, bias[l], state_mlstm_conv[l], w_conv[l], hn[l],
                                             state_mlstm_C[l], state_mlstm_n[l], state_mlstm_m[l][:, None, :])
        h3 = lambda a: a.reshape(bs, N_HEADS, HEAD_DIM)
        fq_s, bq_s = pbf_s[:, 4 * W_MIX:5 * W_MIX], pbf_s[:, 5 * W_MIX:6 * W_MIX]
        f_out_s, lf_s, top = _decode_attn(page_table, cfk, cfv, cbk, lfc, h3(fq_s), h3(bq_s), h3(kv_s[0][l]), h3(kv_s[1][l]),
                                          gate3, bias[l], layer=l, group=DECODE_GROUP)
        blk_pages = (2 * top[:, :N_HEADS, :MOBA_TOPK, None] + jnp.arange(2)).reshape(bs, N_HEADS * MOBA_PAGES)
        b_out_s = _moba_decode(jnp.take_along_axis(page_table, blk_pages, axis=1), cache_moba_k, cache_moba_v,
                               h3(bq_s).astype(F32), h3(kv_s[2][l]), h3(kv_s[3][l]), layer=l)
        flat = lambda a: a.reshape(bs, -1)
        xs = _merge_out(xs, flat(m_out_s), flat(f_out_s), flat(b_out_s), pbf_s, g[3], wbm, wbf, wbb, wo, layer=l, tm=bs)
        xs = _ffn_block(xs, g[4], g[5], wgu2, wd2, layer=l, tm=bs, tf=tf)
        conv_s = jnp.concatenate([state_mlstm_conv[l][:, 1:], pbf_s[:, None, :2 * W_MIX].astype(F32)], axis=1)
        new_s.append((lf_s[:, :, 2 * N_HEADS:3 * N_HEADS], c_s, n_s, m_s[:, 0, :N_HEADS], conv_s))
    logf_p, c_p, n_p, m_p, conv_p = [jnp.stack(z) for z in zip(*new_p)]
    logf_s, c_s, n_s, m_s, conv_s = [jnp.stack(z) for z in zip(*new_s)]
    kv_p = [a.reshape(depth, n_seq, t, N_HEADS, HEAD_DIM) for a in kv_p]
    kv_s = [a.reshape(depth, bs, 1, N_HEADS, HEAD_DIM) for a in kv_s]
    return (xp.reshape(n_seq, t, d), xs.reshape(bs, 1, d),
            kv_p[0], kv_p[1], logf_p, kv_p[2], kv_p[3], c_p, n_p, m_p, conv_p,
            kv_s[0], kv_s[1], logf_s, kv_s[2], kv_s[3], c_s, n_s, m_s, conv_s)
```
